```python
import math
import jax, jax.numpy as jnp
from jax import lax
import numpy as np

D_MODEL = 1024
BATCH = 4
SEQ = 4096
DEPTH = 4

GRID_W = 64
CTX_LEN = 256
N_MIXERS = 3
EXPAND = 2
D_INNER = EXPAND * D_MODEL
EPS = 1e-6
ROPE_BASE = 10000.0
BLOCK = 128

DA_HEADS = D_INNER // 128
DA_HD = 64
DA_VD = 2 * DA_HD

POOL_WINDOWS = (2, 4, 8, 16)
POOL_GROUPS = len(POOL_WINDOWS)
POOL_GW = D_INNER // POOL_GROUPS

WC_HD = 128
WC_HEADS = D_INNER // WC_HD
WC_KV = 4
WC_G = WC_HEADS // WC_KV
WINDOW = 128

N_A = (DEPTH + 2) // 3
N_B = (DEPTH + 1) // 3
N_C = DEPTH // 3

kernel_name = 'hybrid_diffattn_pool_swa_prefix_trunk'


def rmsnorm(x, g):
    xf = x.astype(jnp.float32)
    y = xf * lax.rsqrt(jnp.mean(xf * xf, axis=-1, keepdims=True) + EPS)
    return (y * g.astype(jnp.float32)).astype(x.dtype)


def axial_rope_tables(rows, head_dim):
    quarter = head_dim // 4
    inv = ROPE_BASE ** (-jnp.arange(quarter, dtype=jnp.float32) / quarter)
    row = jnp.repeat(jnp.arange(rows, dtype=jnp.float32), GRID_W)
    col = jnp.tile(jnp.arange(GRID_W, dtype=jnp.float32), rows)
    ar = row[:, None] * inv[None, :]
    ac = col[:, None] * inv[None, :]
    return (jnp.cos(ar), jnp.sin(ar), jnp.cos(ac), jnp.sin(ac))


def apply_axial_rope(t, rope):
    cos_r, sin_r, cos_c, sin_c = rope
    half = t.shape[-1] // 2
    tf = t.astype(jnp.float32)

    def rot(u, cos, sin):
        u1, u2 = jnp.split(u, 2, axis=-1)
        return jnp.concatenate([u1 * cos - u2 * sin, u2 * cos + u1 * sin], axis=-1)

    out = jnp.concatenate([rot(tf[..., :half], cos_r, sin_r), rot(tf[..., half:], cos_c, sin_c)], axis=-1)
    return out.astype(t.dtype)


def adaln(cond, w, b):
    m = jax.nn.silu(cond) @ w + b
    return jnp.split(m, 3, axis=-1)


def _diff_project(t, w_in):
    B, T, _ = t.shape
    q, k, v, z = jnp.split(t @ w_in, 4, axis=-1)
    q = q.reshape(B, T, DA_HEADS, 2, DA_HD).transpose(0, 2, 3, 1, 4)
    k = k.reshape(B, T, DA_HEADS, 2, DA_HD).transpose(0, 2, 3, 1, 4)
    v = v.reshape(B, T, DA_HEADS, DA_VD).transpose(0, 2, 1, 3)
    return q, k, v, z


def diff_attend(q, k, v, lam):
    s = jnp.einsum('bhmqd,bhmkd->bhmqk', q, k).astype(jnp.float32) * (DA_HD ** -0.5)
    p = jax.nn.softmax(s, axis=-1)
    a = p[:, :, 0] - lam * p[:, :, 1]
    return jnp.einsum('bhqk,bhkd->bhqd', a.astype(v.dtype), v)


def _diff_finish(o, z, subln_g, lam_init, w_out):
    B, H, T, VD = o.shape
    o = rmsnorm(o, subln_g) * (1.0 - lam_init)
    o = o.transpose(0, 2, 1, 3).reshape(B, T, H * VD)
    return (o * jax.nn.silu(z)) @ w_out


def diff_attention_mixer(h, hc, w_in, w_out, lq1, lk1, lq2, lk2, subln_g, layer_idx, rows, need_ctx):
    B, S, _ = h.shape
    lam_init = 0.8 - 0.6 * math.exp(-0.3 * layer_idx)
    lam = (jnp.exp(jnp.sum(lq1.astype(jnp.float32) * lk1.astype(jnp.float32)))
           - jnp.exp(jnp.sum(lq2.astype(jnp.float32) * lk2.astype(jnp.float32))) + lam_init)
    rope = axial_rope_tables(rows, DA_HD)
    q, k, v, z = _diff_project(h, w_in)
    q = apply_axial_rope(q, rope)
    k = apply_axial_rope(k, rope)
    qc, kc, vc, zc = _diff_project(hc, w_in)
    k_all = jnp.concatenate([k, kc], axis=3)
    v_all = jnp.concatenate([v, vc], axis=2)
    nb = S // BLOCK
    qb = jnp.moveaxis(q.reshape(B, DA_HEADS, 2, nb, BLOCK, DA_HD), 3, 0)
    ob = lax.map(lambda qi: diff_attend(qi, k_all, v_all, lam), qb)
    o = jnp.moveaxis(ob, 0, 2).reshape(B, DA_HEADS, S, DA_VD)
    y = _diff_finish(o, z, subln_g, lam_init, w_out)
    yc = None
    if need_ctx:
        yc = _diff_finish(diff_attend(qc, kc, vc, lam), zc, subln_g, lam_init, w_out)
    return y, yc


def centred_mean(u, w):
    T = u.shape[1]
    cs = jnp.pad(jnp.cumsum(u.astype(jnp.float32), axis=1), ((0, 0), (1, 0), (0, 0)))
    t = jnp.arange(T)
    lo = jnp.clip(t - w // 2, 0, T)
    hi = jnp.clip(t - w // 2 + w, 0, T)
    total = cs[:, hi] - cs[:, lo]
    cnt = (hi - lo).astype(jnp.float32)
    return (total / cnt[None, :, None]).astype(u.dtype)


def pool_mixer(h, w_in, w_grp, b_grp, scale, w_out):
    B, T, _ = h.shape
    u, z = jnp.split(h @ w_in, 2, axis=-1)
    ug = u.reshape(B, T, POOL_GROUPS, POOL_GW)
    pooled = jnp.stack([centred_mean(ug[:, :, g], w) for g, w in enumerate(POOL_WINDOWS)], axis=2)
    d = pooled - ug
    y = jnp.einsum('btgc,gcd->btgd', d, w_grp) + b_grp.reshape(POOL_GROUPS, POOL_GW)
    y = y.reshape(B, T, D_INNER) * scale
    return (y * jax.nn.silu(z)) @ w_out


def _gqa_project(t, w_in):
    B, T, _ = t.shape
    kvw = WC_KV * WC_HD
    q, k, v, z = jnp.split(t @ w_in, [D_INNER, D_INNER + kvw, D_INNER + 2 * kvw], axis=-1)
    q = q.reshape(B, T, WC_KV, WC_G, WC_HD).transpose(0, 2, 3, 1, 4)
    k = k.reshape(B, T, WC_KV, WC_HD).transpose(0, 2, 1, 3)
    v = v.reshape(B, T, WC_KV, WC_HD).transpose(0, 2, 1, 3)
    return q, k, v, z


def sink_attend(q, k, v, sink, mask):
    s = jnp.einsum('bngqd,bnkd->bngqk', q, k).astype(jnp.float32) * (WC_HD ** -0.5)
    if mask is not None:
        s = jnp.where(mask, s, -jnp.inf)
    sk = sink.astype(jnp.float32).reshape(1, WC_KV, WC_G, 1, 1)
    m = jnp.maximum(jnp.max(s, axis=-1, keepdims=True), sk)
    e = jnp.exp(s - m)
    p = e / (jnp.sum(e, axis=-1, keepdims=True) + jnp.exp(sk - m))
    return jnp.einsum('bngqk,bnkd->bngqd', p.astype(v.dtype), v)


def _gqa_finish(o, z, w_out):
    B, KV, G, T, HD = o.shape
    o = o.transpose(0, 3, 1, 2, 4).reshape(B, T, KV * G * HD)
    return (o * jax.nn.silu(z)) @ w_out


def window_gqa_mixer(h, hc, w_in, sink, w_out, rows, need_ctx):
    B, S, _ = h.shape
    rope = axial_rope_tables(rows, WC_HD)
    q, k, v, z = _gqa_project(h, w_in)
    q = apply_axial_rope(q, rope)
    k = apply_axial_rope(k, rope)
    qc, kc, vc, zc = _gqa_project(hc, w_in)
    nb = S // BLOCK
    pad = ((0, 0), (0, 0), (BLOCK, BLOCK), (0, 0))
    kp = jnp.pad(k, pad)
    vp = jnp.pad(v, pad)
    qb = jnp.moveaxis(q.reshape(B, WC_KV, WC_G, nb, BLOCK, WC_HD), 3, 0)
    ctx_valid = jnp.ones((BLOCK, kc.shape[2]), dtype=bool)

    def band_block(args):
        qi, bi = args
        start = bi * BLOCK
        kb = lax.dynamic_slice_in_dim(kp, start, 3 * BLOCK, axis=2)
        vb = lax.dynamic_slice_in_dim(vp, start, 3 * BLOCK, axis=2)
        qpos = start + jnp.arange(BLOCK)
        kpos = start - BLOCK + jnp.arange(3 * BLOCK)
        valid = ((jnp.abs(qpos[:, None] - kpos[None, :]) <= WINDOW)
                 & (kpos >= 0)[None, :] & (kpos < S)[None, :])
        mask = jnp.concatenate([valid, ctx_valid], axis=1)
        return sink_attend(qi, jnp.concatenate([kb, kc], axis=2), jnp.concatenate([vb, vc], axis=2), sink, mask)

    ob = lax.map(band_block, (qb, jnp.arange(nb)))
    o = jnp.moveaxis(ob, 0, 3).reshape(B, WC_KV, WC_G, S, WC_HD)
    y = _gqa_finish(o, z, w_out)
    yc = None
    if need_ctx:
        yc = _gqa_finish(sink_attend(qc, kc, vc, sink, None), zc, w_out)
    return y, yc


def setup_inputs(seed: int = 0) -> dict:
    key = jax.random.key(seed)
    ks = jax.random.split(key, 24)
    f32 = jnp.float32
    nrm = lambda k, shape, s: jax.random.normal(k, shape, f32) * s
    w_in_a_cols = 4 * D_INNER
    w_in_c_cols = 2 * D_INNER + 2 * WC_KV * WC_HD
    return {
        'x': nrm(ks[0], (BATCH, SEQ, D_MODEL), 1.0),
        'c': nrm(ks[1], (BATCH, D_MODEL), 1.0),
        'ctx': nrm(ks[2], (BATCH, CTX_LEN, D_MODEL), 1.0),
        'c_ctx': nrm(ks[3], (D_MODEL,), 1.0),
        'norm_g': 1.0 + nrm(ks[4], (DEPTH, D_MODEL), 0.05),
        'w_ada': nrm(ks[5], (DEPTH, D_MODEL, 3 * D_MODEL), 0.5 * D_MODEL ** -0.5),
        'b_ada': nrm(ks[6], (DEPTH, 3 * D_MODEL), 0.02),
        'a_w_in': nrm(ks[7], (N_A, D_MODEL, w_in_a_cols), D_MODEL ** -0.5),
        'a_w_out': nrm(ks[8], (N_A, D_INNER, D_MODEL), D_INNER ** -0.5),
        'a_lam_q1': nrm(ks[9], (N_A, DA_HD), 0.1),
        'a_lam_k1': nrm(ks[10], (N_A, DA_HD), 0.1),
        'a_lam_q2': nrm(ks[11], (N_A, DA_HD), 0.1),
        'a_lam_k2': nrm(ks[12], (N_A, DA_HD), 0.1),
        'a_subln_g': 1.0 + nrm(ks[13], (N_A, DA_VD), 0.05),
        'b_w_in': nrm(ks[14], (N_B, D_MODEL, 2 * D_INNER), D_MODEL ** -0.5),
        'b_w_grp': nrm(ks[15], (N_B, POOL_GROUPS, POOL_GW, POOL_GW), POOL_GW ** -0.5),
        'b_b_grp': nrm(ks[16], (N_B, D_INNER), 0.02),
        'b_scale': 1.0 + nrm(ks[17], (N_B, D_INNER), 0.1),
        'b_w_out': nrm(ks[18], (N_B, D_INNER, D_MODEL), D_INNER ** -0.5),
        'c_w_in': nrm(ks[19], (N_C, D_MODEL, w_in_c_cols), D_MODEL ** -0.5),
        'c_sink': nrm(ks[20], (N_C, WC_HEADS), 1.0),
        'c_w_out': nrm(ks[21], (N_C, D_INNER, D_MODEL), D_INNER ** -0.5),
        'final_g': 1.0 + nrm(ks[22], (D_MODEL,), 0.05),
    }


def reference(x, c, ctx, c_ctx, norm_g, w_ada, b_ada,
              a_w_in, a_w_out, a_lam_q1, a_lam_k1, a_lam_q2, a_lam_k2, a_subln_g,
              b_w_in, b_w_grp, b_b_grp, b_scale, b_w_out,
              c_w_in, c_sink, c_w_out, final_g):
    S = x.shape[1]
    ROWS = S // GRID_W
    xc = ctx
    for i in range(DEPTH):
        m = i % N_MIXERS
        j = i // N_MIXERS
        need_ctx = i < DEPTH - 1
        sh, sc, gt = adaln(c, w_ada[i], b_ada[i])
        h = rmsnorm(x, norm_g[i]) * (1.0 + sc[:, None, :]) + sh[:, None, :]
        if need_ctx or m != 1:
            csh, csc, cgt = adaln(c_ctx, w_ada[i], b_ada[i])
            hc = rmsnorm(xc, norm_g[i]) * (1.0 + csc) + csh
        if m == 0:
            y, yc = diff_attention_mixer(h, hc, a_w_in[j], a_w_out[j], a_lam_q1[j], a_lam_k1[j],
                                         a_lam_q2[j], a_lam_k2[j], a_subln_g[j], i, ROWS, need_ctx)
        elif m == 1:
            y = pool_mixer(h, b_w_in[j], b_w_grp[j], b_b_grp[j], b_scale[j], b_w_out[j])
            yc = pool_mixer(hc, b_w_in[j], b_w_grp[j], b_b_grp[j], b_scale[j], b_w_out[j]) if need_ctx else None
        else:
            y, yc = window_gqa_mixer(h, hc, c_w_in[j], c_sink[j], c_w_out[j], ROWS, need_ctx)
        x = x + gt[:, None, :] * y
        if need_ctx:
            xc = xc + cgt * yc
    return rmsnorm(x, final_g)
```

```python
import functools
import math

import jax
import jax.numpy as jnp
from jax import lax
from jax.experimental import pallas as pl
from jax.experimental.pallas import tpu as pltpu

F32 = jnp.float32
BF16 = jnp.bfloat16

GRID_W = 64
EPS = 1e-6
ROPE_BASE = 10000.0
N_MIXERS = 3
DA_HEADS = 16
DA_HD = 64
DA_VD = 128
POOL_WINDOWS = (2, 4, 8, 16)
POOL_GW = 512
WC_HD = 128
WC_KV = 4
WC_G = 4
WINDOW = 128

LOG2E = 1.4426950408889634
LANE = 128
MIB = 1024 * 1024


def _params(semantics, vmem_mib):
    return pltpu.CompilerParams(dimension_semantics=semantics,
                                vmem_limit_bytes=vmem_mib * MIB)


def _silu(v):
    return v * jax.nn.sigmoid(v)


def _adaln_kernel(c_ref, w_ref, b_ref, o_ref):
    a = _silu(c_ref[...]).astype(BF16)
    o_ref[0] = jnp.dot(a, w_ref[0].astype(BF16), preferred_element_type=F32) + b_ref[0]


def _adaln(cond8, w_ada, b_ada):
    depth, d, d3 = w_ada.shape
    nt = d3 // d
    return pl.pallas_call(
        _adaln_kernel,
        grid=(depth, nt),
        in_specs=[
            pl.BlockSpec((8, d), lambda l, n: (0, 0)),
            pl.BlockSpec((1, d, d), lambda l, n: (l, 0, n)),
            pl.BlockSpec((1, 1, d), lambda l, n: (l, 0, n)),
        ],
        out_specs=pl.BlockSpec((1, 8, d), lambda l, n: (l, 0, n)),
        out_shape=jax.ShapeDtypeStruct((depth, 8, d3), F32),
        compiler_params=_params(("parallel", "parallel"), 32),
        name="adaln",
    )(cond8, w_ada, b_ada.reshape(depth, 1, d3))


def _norm_mod(xf, g, shift, scale):
    y = xf * lax.rsqrt(jnp.mean(xf * xf, axis=-1, keepdims=True) + EPS) * g
    return y * (1.0 + scale) + shift


def _proj_kernel(x_ref, mod_ref, g_ref, w_ref, cos_ref, sin_ref, o_ref, h_ref,
                 *, tm, tn, seq, n_rope, n_q, qscale, rope_blk):
    i = pl.program_id(1)
    j = pl.program_id(2)

    @pl.when(j == 0)
    def _():
        row = i * tm + lax.broadcasted_iota(jnp.int32, (tm, 1), 0)
        is_ctx = row >= seq
        shift = jnp.where(is_ctx, mod_ref[0, 3:4, :], mod_ref[0, 0:1, :])
        scale = jnp.where(is_ctx, mod_ref[0, 4:5, :], mod_ref[0, 1:2, :])
        h_ref[...] = _norm_mod(x_ref[0], g_ref[...], shift, scale).astype(BF16)

    acc = jnp.dot(h_ref[...], w_ref[...], preferred_element_type=F32)

    @pl.when(j < n_rope)
    def _():
        lane = lax.broadcasted_iota(jnp.int32, (1, LANE), 1)
        first_half = (lane // rope_blk) % 2 == 0
        mul = jnp.where(j < n_q, qscale, 1.0).astype(F32)
        cos = cos_ref[...] * mul
        sin = sin_ref[...] * mul
        for s in range(tn // LANE):
            t = acc[:, s * LANE:(s + 1) * LANE]
            partner = jnp.where(first_half,
                                pltpu.roll(t, LANE - rope_blk, 1),
                                pltpu.roll(t, rope_blk, 1))
            o_ref[0, :, s * LANE:(s + 1) * LANE] = (t * cos + partner * sin).astype(o_ref.dtype)

    @pl.when(j >= n_rope)
    def _():
        o_ref[0] = acc.astype(o_ref.dtype)


def _proj(xa, mod, g, w, cos, sin, *, seq, n_rope, n_q, qscale, rope_blk, tm=1088, tn=512):
    b, t, d = xa.shape
    n = w.shape[1]
    kern = functools.partial(_proj_kernel, tm=tm, tn=tn, seq=seq, n_rope=n_rope, n_q=n_q,
                             qscale=qscale, rope_blk=rope_blk)
    return pl.pallas_call(
        kern,
        grid=(b, t // tm, n // tn),
        in_specs=[
            pl.BlockSpec((1, tm, d), lambda bi, i, j: (bi, i, 0)),
            pl.BlockSpec((1, 8, d), lambda bi, i, j: (bi, 0, 0)),
            pl.BlockSpec((1, d), lambda bi, i, j: (0, 0)),
            pl.BlockSpec((d, tn), lambda bi, i, j: (0, j)),
            pl.BlockSpec((tm, LANE), lambda bi, i, j: (i, 0)),
            pl.BlockSpec((tm, LANE), lambda bi, i, j: (i, 0)),
        ],
        out_specs=pl.BlockSpec((1, tm, tn), lambda bi, i, j: (bi, i, j)),
        out_shape=jax.ShapeDtypeStruct((b, t, n), BF16),
        scratch_shapes=[pltpu.VMEM((tm, d), BF16)],
        compiler_params=_params(("parallel", "parallel", "arbitrary"), 48),
        name="proj",
    )(xa, mod, g, w, cos, sin)


def _rope_tables(seq, n_ctx, head_dim):
    quarter = head_dim // 4
    inv = ROPE_BASE ** (-jnp.arange(quarter, dtype=F32) / quarter)
    pos = jnp.arange(seq)
    ar = (pos // GRID_W).astype(F32)[:, None] * inv[None, :]
    ac = (pos % GRID_W).astype(F32)[:, None] * inv[None, :]
    cos = jnp.concatenate([jnp.cos(ar), jnp.cos(ar), jnp.cos(ac), jnp.cos(ac)], axis=-1)
    sin = jnp.concatenate([-jnp.sin(ar), jnp.sin(ar), -jnp.sin(ac), jnp.sin(ac)], axis=-1)
    reps = LANE // head_dim
    cos = jnp.tile(cos, (1, reps))
    sin = jnp.tile(sin, (1, reps))
    cos = jnp.concatenate([cos, jnp.ones((n_ctx, LANE), F32)], axis=0)
    sin = jnp.concatenate([sin, jnp.zeros((n_ctx, LANE), F32)], axis=0)
    return cos, sin


def _diff_attend(qm, k_ref, vt_ref, chunks):
    m = l = acc = None
    for c0, cl in chunks:
        s = lax.dot_general(k_ref[0, c0:c0 + cl, :], qm, (((1,), (1,)), ((), ())),
                            preferred_element_type=F32)
        mc = jnp.max(s, axis=0, keepdims=True)
        m_new = mc if m is None else jnp.maximum(m, mc)
        e = jnp.exp2(s - m_new)
        ls = jnp.sum(e, axis=0, keepdims=True)
        pv = jnp.dot(vt_ref[0, 0, :, c0:c0 + cl], e.astype(BF16),
                     preferred_element_type=F32)
        if m is None:
            l, acc = ls, pv
        else:
            alpha = jnp.exp2(m - m_new)
            l = alpha * l + ls
            acc = alpha * acc + pv
        m = m_new
    return acc, l


def _diff_attn_kernel(lamp_ref, q_ref, k_ref, vt_ref, z_ref, sg_ref, o_ref,
                      *, tq, seq, n_lat_tiles, lat_chunks, ctx_chunks, lam_init):
    i = pl.program_id(2)
    q = q_ref[0].astype(F32)
    lane = lax.broadcasted_iota(jnp.int32, q.shape, 1)
    qm = jnp.concatenate([jnp.where(lane < DA_HD, q, 0.0),
                          jnp.where(lane >= DA_HD, q, 0.0)], axis=0).astype(BF16)
    lp = lamp_ref[...]
    lam = (jnp.exp(jnp.sum(lp[0:1] * lp[1:2], axis=1, keepdims=True))
           - jnp.exp(jnp.sum(lp[2:3] * lp[3:4], axis=1, keepdims=True)) + lam_init)

    def finish(acc, l):
        inv = 1.0 / l
        o_t = acc[:, :tq] * inv[:, :tq] - lam * (acc[:, tq:] * inv[:, tq:])
        o = o_t.T
        o = o * lax.rsqrt(jnp.mean(o * o, axis=-1, keepdims=True) + EPS) * sg_ref[...]
        o = o * (1.0 - lam_init)
        o_ref[0] = (o * _silu(z_ref[0].astype(F32))).astype(o_ref.dtype)

    @pl.when(i < n_lat_tiles)
    def _():
        finish(*_diff_attend(qm, k_ref, vt_ref, lat_chunks))

    if ctx_chunks is not None:
        @pl.when(i >= n_lat_tiles)
        def _():
            finish(*_diff_attend(qm, k_ref, vt_ref, ctx_chunks))


def _diff_attn(qkvz, vt, lamp, sg, *, seq, n_ctx, lam_init, with_ctx, tq=256, tk=512):
    b, t, _ = qkvz.shape
    h = DA_HEADS
    n_lat = seq // tq
    nq = n_lat + (n_ctx // tq if with_ctx else 0)
    lat_chunks = tuple((c, tk) for c in range(0, seq, tk)) + ((seq, n_ctx),)
    ctx_chunks = ((seq, n_ctx),) if with_ctx else None
    kern = functools.partial(_diff_attn_kernel, tq=tq, seq=seq, n_lat_tiles=n_lat,
                             lat_chunks=lat_chunks, ctx_chunks=ctx_chunks, lam_init=lam_init)
    out_rows = t if with_ctx else seq
    return pl.pallas_call(
        kern,
        grid=(b, h, nq),
        in_specs=[
            pl.BlockSpec((4, DA_HD), lambda bi, hi, i: (0, 0)),
            pl.BlockSpec((1, tq, LANE), lambda bi, hi, i: (bi, i, hi)),
            pl.BlockSpec((1, t, LANE), lambda bi, hi, i: (bi, 0, h + hi)),
            pl.BlockSpec((1, 1, DA_VD, t), lambda bi, hi, i: (bi, hi, 0, 0)),
            pl.BlockSpec((1, tq, LANE), lambda bi, hi, i: (bi, i, 3 * h + hi)),
            pl.BlockSpec((1, DA_VD), lambda bi, hi, i: (0, 0)),
        ],
        out_specs=pl.BlockSpec((1, tq, LANE), lambda bi, hi, i: (bi, i, hi)),
        out_shape=jax.ShapeDtypeStruct((b, out_rows, h * DA_VD), BF16),
        compiler_params=_params(("parallel", "parallel", "arbitrary"), 48),
        name="diff_attn",
    )(lamp, qkvz, qkvz, vt, qkvz, sg)


def _swa_kernel(q_ref, k_ref, vt_ref, z_ref, sink_ref, o_ref, *, tq, seq, n_ctx, n_lat_tiles):
    i = pl.program_id(2)
    span = tq + 2 * WINDOW
    q4 = jnp.concatenate([q_ref[0, :, g * WC_HD:(g + 1) * WC_HD] for g in range(WC_G)],
                         axis=0)
    sink = sink_ref[0]
    nt = (((1,), (1,)), ((), ()))
    k_ctx = k_ref[0, seq:seq + n_ctx, :]
    vt_ctx = vt_ref[0, 0, :, seq:seq + n_ctx]
    s_ctx = lax.dot_general(k_ctx, q4, nt, preferred_element_type=F32)
    m_ctx = jnp.maximum(jnp.max(s_ctx, axis=0, keepdims=True), sink)

    def finish(o_t, den):
        o_t = o_t * (1.0 / den)
        for g in range(WC_G):
            o = o_t[:, g * tq:(g + 1) * tq].T
            z = z_ref[0, :, g * WC_HD:(g + 1) * WC_HD].astype(F32)
            o_ref[0, :, g * WC_HD:(g + 1) * WC_HD] = (o * _silu(z)).astype(o_ref.dtype)

    @pl.when(i < n_lat_tiles)
    def _():
        start = i * tq
        base = pl.multiple_of(jnp.clip(start - WINDOW, 0, seq - span), LANE)
        s_lat = lax.dot_general(k_ref[0, pl.ds(base, span), :], q4, nt,
                                preferred_element_type=F32)
        kpos = base + lax.broadcasted_iota(jnp.int32, (span, 1), 0)
        qpos = start + lax.rem(lax.broadcasted_iota(jnp.int32, (1, WC_G * tq), 1), tq)
        s_lat = jnp.where(jnp.abs(qpos - kpos) <= WINDOW, s_lat, -jnp.inf)
        m = jnp.maximum(m_ctx, jnp.max(s_lat, axis=0, keepdims=True))
        e_lat = jnp.exp2(s_lat - m)
        e_ctx = jnp.exp2(s_ctx - m)
        den = (jnp.sum(e_lat, axis=0, keepdims=True) + jnp.sum(e_ctx, axis=0, keepdims=True)
               + jnp.exp2(sink - m))
        o_t = (jnp.dot(vt_ref[0, 0, :, pl.ds(base, span)], e_lat.astype(BF16),
                       preferred_element_type=F32)
               + jnp.dot(vt_ctx, e_ctx.astype(BF16), preferred_element_type=F32))
        finish(o_t, den)

    @pl.when(i >= n_lat_tiles)
    def _():
        e_ctx = jnp.exp2(s_ctx - m_ctx)
        den = jnp.sum(e_ctx, axis=0, keepdims=True) + jnp.exp2(sink - m_ctx)
        finish(jnp.dot(vt_ctx, e_ctx.astype(BF16), preferred_element_type=F32), den)


def _swa_attn(qkvz, vt, sink_rows, *, seq, n_ctx, tq=256):
    b, t, _ = qkvz.shape
    gw = WC_G * WC_HD
    n_lat = seq // tq
    kern = functools.partial(_swa_kernel, tq=tq, seq=seq, n_ctx=n_ctx, n_lat_tiles=n_lat)
    k_blk0 = WC_KV * gw // WC_HD
    z_blk0 = (WC_KV * gw + 2 * WC_KV * WC_HD) // gw
    return pl.pallas_call(
        kern,
        grid=(b, WC_KV, t // tq),
        in_specs=[
            pl.BlockSpec((1, tq, gw), lambda bi, n, i: (bi, i, n)),
            pl.BlockSpec((1, t, WC_HD), lambda bi, n, i: (bi, 0, k_blk0 + n)),
            pl.BlockSpec((1, 1, WC_HD, t), lambda bi, n, i: (bi, n, 0, 0)),
            pl.BlockSpec((1, tq, gw), lambda bi, n, i: (bi, i, z_blk0 + n)),
            pl.BlockSpec((1, 1, WC_G * tq), lambda bi, n, i: (n, 0, 0)),
        ],
        out_specs=pl.BlockSpec((1, tq, gw), lambda bi, n, i: (bi, i, n)),
        out_shape=jax.ShapeDtypeStruct((b, t, WC_KV * gw), BF16),
        compiler_params=_params(("parallel", "parallel", "arbitrary"), 48),
        name="swa_attn",
    )(qkvz, qkvz, vt, qkvz, sink_rows)


HALO = 8


def _pool_kernel(x_ref, xp_ref, xn_ref, mod_ref, g_ref, w_ref, wg_ref, bg_ref, sc_ref,
                 o_ref, hb_ref, s2_ref, s4_ref, s8_ref, *, tm, seq, n_ctx, n_lat_tiles):
    i = pl.program_id(1)
    is_ctx = i >= n_lat_tiles
    shift = jnp.where(is_ctx, mod_ref[0, 3:4, :], mod_ref[0, 0:1, :])
    scale = jnp.where(is_ctx, mod_ref[0, 4:5, :], mod_ref[0, 1:2, :])
    g = g_ref[...]
    has_prev = jnp.logical_and(i != 0, i != n_lat_tiles)
    has_next = jnp.logical_and(i != n_lat_tiles - 1, i < n_lat_tiles)
    h = _norm_mod(x_ref[0], g, shift, scale)
    hb_ref[HALO:HALO + tm, :] = h
    hb_ref[0:HALO, :] = jnp.where(has_prev, _norm_mod(xp_ref[0], g, shift, scale), 0.0)
    hb_ref[HALO + tm:, :] = jnp.where(has_next, _norm_mod(xn_ref[0], g, shift, scale), 0.0)
    n = tm + 2 * HALO
    s2_ref[1:n, :] = hb_ref[0:n - 1, :] + hb_ref[1:n, :]
    s4_ref[2:n - 1, :] = s2_ref[1:n - 2, :] + s2_ref[3:n, :]
    s8_ref[4:n - 3, :] = s4_ref[2:n - 5, :] + s4_ref[6:n - 1, :]
    sums = (s2_ref[HALO:HALO + tm, :], s4_ref[HALO:HALO + tm, :], s8_ref[HALO:HALO + tm, :],
            s8_ref[HALO - 4:HALO - 4 + tm, :] + s8_ref[HALO + 4:HALO + 4 + tm, :])
    pos = jnp.where(is_ctx, 0, i * tm) + lax.broadcasted_iota(jnp.int32, (tm, 1), 0)
    seg_len = jnp.where(is_ctx, n_ctx, seq)
    hb16 = h.astype(BF16)
    d_inner = len(POOL_WINDOWS) * POOL_GW
    for gi, w in enumerate(POOL_WINDOWS):
        lo = jnp.maximum(pos - w // 2, 0)
        hi = jnp.minimum(pos - w // 2 + w, seg_len)
        cnt = (hi - lo).astype(F32)
        hd = (sums[gi] / cnt - h).astype(BF16)
        cols = slice(gi * POOL_GW, (gi + 1) * POOL_GW)
        d = jnp.dot(hd, w_ref[:, cols], preferred_element_type=F32)
        y = jnp.dot(d.astype(BF16), wg_ref[gi], preferred_element_type=F32) + bg_ref[:, cols]
        y = y * sc_ref[:, cols]
        z = jnp.dot(hb16, w_ref[:, d_inner + gi * POOL_GW:d_inner + (gi + 1) * POOL_GW],
                    preferred_element_type=F32)
        o_ref[0, :, cols] = (y * _silu(z)).astype(o_ref.dtype)


def _pool_mixer(xa, mod, g, w_in, w_grp, b_grp, scale, *, seq, n_ctx, tm=256):
    b, t, d = xa.shape
    d_inner = len(POOL_WINDOWS) * POOL_GW
    hb = tm // HALO
    last = t // HALO - 1
    kern = functools.partial(_pool_kernel, tm=tm, seq=seq, n_ctx=n_ctx, n_lat_tiles=seq // tm)
    buf = pltpu.VMEM((tm + 2 * HALO, d), F32)
    return pl.pallas_call(
        kern,
        grid=(b, t // tm),
        in_specs=[
            pl.BlockSpec((1, tm, d), lambda bi, i: (bi, i, 0)),
            pl.BlockSpec((1, HALO, d), lambda bi, i: (bi, jnp.maximum(i * hb - 1, 0), 0)),
            pl.BlockSpec((1, HALO, d), lambda bi, i: (bi, jnp.minimum((i + 1) * hb, last), 0)),
            pl.BlockSpec((1, 8, d), lambda bi, i: (bi, 0, 0)),
            pl.BlockSpec((1, d), lambda bi, i: (0, 0)),
            pl.BlockSpec((d, 2 * d_inner), lambda bi, i: (0, 0)),
            pl.BlockSpec((len(POOL_WINDOWS), POOL_GW, POOL_GW), lambda bi, i: (0, 0, 0)),
            pl.BlockSpec((1, d_inner), lambda bi, i: (0, 0)),
            pl.BlockSpec((1, d_inner), lambda bi, i: (0, 0)),
        ],
        out_specs=pl.BlockSpec((1, tm, d_inner), lambda bi, i: (bi, i, 0)),
        out_shape=jax.ShapeDtypeStruct((b, t, d_inner), BF16),
        scratch_shapes=[buf, buf, buf, buf],
        compiler_params=_params(("parallel", "arbitrary"), 56),
        name="pool_mixer",
    )(xa, xa, xa, mod, g, w_in, w_grp, b_grp, scale)


def _out_kernel(g_ref, w_ref, x_ref, mod_ref, fg_ref, o_ref, *, tm, seq, final):
    i = pl.program_id(1)
    y = jnp.dot(g_ref[0], w_ref[...], preferred_element_type=F32)
    row = i * tm + lax.broadcasted_iota(jnp.int32, (tm, 1), 0)
    gate = jnp.where(row >= seq, mod_ref[0, 5:6, :], mod_ref[0, 2:3, :])
    xn = x_ref[0] + gate * y
    if final:
        xn = xn * lax.rsqrt(jnp.mean(xn * xn, axis=-1, keepdims=True) + EPS) * fg_ref[...]
    o_ref[0] = xn


def _out_proj(gact, w, xa, mod, final_g, *, seq, final, tm):
    b, _, d_inner = gact.shape
    d = w.shape[1]
    rows = seq if final else xa.shape[1]
    kern = functools.partial(_out_kernel, tm=tm, seq=seq, final=final)
    return pl.pallas_call(
        kern,
        grid=(b, rows // tm),
        in_specs=[
            pl.BlockSpec((1, tm, d_inner), lambda bi, i: (bi, i, 0)),
            pl.BlockSpec((d_inner, d), lambda bi, i: (0, 0)),
            pl.BlockSpec((1, tm, d), lambda bi, i: (bi, i, 0)),
            pl.BlockSpec((1, 8, d), lambda bi, i: (bi, 0, 0)),
            pl.BlockSpec((1, d), lambda bi, i: (0, 0)),
        ],
        out_specs=pl.BlockSpec((1, tm, d), lambda bi, i: (bi, i, 0)),
        out_shape=jax.ShapeDtypeStruct((b, rows, d), F32),
        compiler_params=_params(("parallel", "parallel"), 48),
        name="out_proj",
    )(gact, w, xa, mod, final_g)


def kernel(x, c, ctx, c_ctx, norm_g, w_ada, b_ada, a_w_in, a_w_out, a_lam_q1, a_lam_k1, a_lam_q2, a_lam_k2, a_subln_g, b_w_in, b_w_grp, b_b_grp, b_scale, b_w_out, c_w_in, c_sink, c_w_out, final_g):
    b, seq, d = x.shape
    n_ctx = ctx.shape[1]
    depth = w_ada.shape[0]
    xa = jnp.concatenate([x, ctx], axis=1)

    cond8 = jnp.concatenate([c, c_ctx[None, :], jnp.zeros((8 - b - 1, d), F32)], axis=0)
    ada = _adaln(cond8, w_ada, b_ada)

    cos_a, sin_a = _rope_tables(seq, n_ctx, DA_HD)
    cos_c, sin_c = _rope_tables(seq, n_ctx, WC_HD)

    out = None
    for i in range(depth):
        m = i % N_MIXERS
        j = i // N_MIXERS
        last = i == depth - 1
        lat = ada[i, :b].reshape(b, 3, d)
        cm = jnp.broadcast_to(ada[i, b].reshape(1, 3, d), (b, 3, d))
        mod = jnp.concatenate([lat, cm, jnp.zeros((b, 2, d), F32)], axis=1)
        g = norm_g[i].reshape(1, d)

        if m == 0:
            lam_init = 0.8 - 0.6 * math.exp(-0.3 * i)
            qkvz = _proj(xa, mod, g, a_w_in[j].astype(BF16), cos_a, sin_a, seq=seq,
                         n_rope=8, n_q=4, qscale=DA_HD ** -0.5 * LOG2E, rope_blk=DA_HD // 4)
            v0 = 2 * DA_HEADS * DA_VD
            vt = qkvz[:, :, v0:v0 + DA_HEADS * DA_VD].reshape(b, -1, DA_HEADS, DA_VD)
            vt = vt.transpose(0, 2, 3, 1)
            lamp = jnp.stack([a_lam_q1[j], a_lam_k1[j], a_lam_q2[j], a_lam_k2[j]], axis=0)
            gact = _diff_attn(qkvz, vt, lamp, a_subln_g[j].reshape(1, DA_VD), seq=seq,
                              n_ctx=n_ctx, lam_init=lam_init, with_ctx=not last)
            w_out = a_w_out[j]
        elif m == 1:
            gact = _pool_mixer(xa, mod, g, b_w_in[j].astype(BF16), b_w_grp[j].astype(BF16),
                               b_b_grp[j].reshape(1, -1), b_scale[j].reshape(1, -1),
                               seq=seq, n_ctx=n_ctx)
            w_out = b_w_out[j]
        else:
            qkvz = _proj(xa, mod, g, c_w_in[j].astype(BF16), cos_c, sin_c, seq=seq,
                         n_rope=5, n_q=4, qscale=WC_HD ** -0.5 * LOG2E, rope_blk=WC_HD // 4)
            v0 = WC_KV * WC_G * WC_HD + WC_KV * WC_HD
            vt = qkvz[:, :, v0:v0 + WC_KV * WC_HD].reshape(b, -1, WC_KV, WC_HD)
            vt = vt.transpose(0, 2, 3, 1)
            tq = 256
            sink_rows = jnp.repeat(c_sink[j].astype(F32) * LOG2E, tq).reshape(WC_KV, 1, WC_G * tq)
            gact = _swa_attn(qkvz, vt, sink_rows, seq=seq, n_ctx=n_ctx, tq=tq)
            w_out = c_w_out[j]

        res = _out_proj(gact, w_out.astype(BF16), xa, mod, final_g.reshape(1, d), seq=seq,
                        final=last, tm=512 if last else 544)
        if last:
            out = res
        else:
            xa = res
    return out
```

```python
import functools
import math

import jax
import jax.numpy as jnp
from jax import lax
from jax.experimental import pallas as pl
from jax.experimental.pallas import tpu as pltpu

F32 = jnp.float32
BF16 = jnp.bfloat16

GRID_W = 64
EPS = 1e-6
ROPE_BASE = 10000.0
N_MIXERS = 3
DA_HEADS = 16
DA_HD = 64
DA_VD = 128
POOL_WINDOWS = (2, 4, 8, 16)
POOL_GW = 512
WC_HD = 128
WC_KV = 4
WC_G = 4
WINDOW = 128

LOG2E = 1.4426950408889634
LANE = 128
MIB = 1024 * 1024


def _params(semantics, vmem_mib):
    return pltpu.CompilerParams(dimension_semantics=semantics,
                                vmem_limit_bytes=vmem_mib * MIB)


def _silu(v):
    return v * jax.nn.sigmoid(v)


def _adaln_kernel(c_ref, w_ref, b_ref, o_ref):
    a = _silu(c_ref[...]).astype(BF16)
    o_ref[0] = jnp.dot(a, w_ref[0].astype(BF16), preferred_element_type=F32) + b_ref[0]


def _adaln(cond8, w_ada, b_ada):
    depth, d, d3 = w_ada.shape
    nt = d3 // d
    return pl.pallas_call(
        _adaln_kernel,
        grid=(depth, nt),
        in_specs=[
            pl.BlockSpec((8, d), lambda l, n: (0, 0)),
            pl.BlockSpec((1, d, d), lambda l, n: (l, 0, n)),
            pl.BlockSpec((1, 1, d), lambda l, n: (l, 0, n)),
        ],
        out_specs=pl.BlockSpec((1, 8, d), lambda l, n: (l, 0, n)),
        out_shape=jax.ShapeDtypeStruct((depth, 8, d3), F32),
        compiler_params=_params(("parallel", "parallel"), 32),
        name="adaln",
    )(cond8, w_ada, b_ada.reshape(depth, 1, d3))


def _norm_mod(xf, g, shift, scale):
    y = xf * lax.rsqrt(jnp.mean(xf * xf, axis=-1, keepdims=True) + EPS) * g
    return y * (1.0 + scale) + shift


def _proj_kernel(x_ref, mod_ref, g_ref, w_ref, cos_ref, sin_ref, o_ref, h_ref,
                 *, tm, tn, seq, n_rope, n_q, qscale, rope_blk):
    i = pl.program_id(1)
    j = pl.program_id(2)

    @pl.when(j == 0)
    def _():
        row = i * tm + lax.broadcasted_iota(jnp.int32, (tm, 1), 0)
        is_ctx = row >= seq
        shift = jnp.where(is_ctx, mod_ref[0, 3:4, :], mod_ref[0, 0:1, :])
        scale = jnp.where(is_ctx, mod_ref[0, 4:5, :], mod_ref[0, 1:2, :])
        h_ref[...] = _norm_mod(x_ref[0], g_ref[...], shift, scale).astype(BF16)

    acc = jnp.dot(h_ref[...], w_ref[...], preferred_element_type=F32)

    @pl.when(j < n_rope)
    def _():
        lane = lax.broadcasted_iota(jnp.int32, (1, LANE), 1)
        first_half = (lane // rope_blk) % 2 == 0
        mul = jnp.where(j < n_q, qscale, 1.0).astype(F32)
        cos = cos_ref[...] * mul
        sin = sin_ref[...] * mul
        for s in range(tn // LANE):
            t = acc[:, s * LANE:(s + 1) * LANE]
            partner = jnp.where(first_half,
                                pltpu.roll(t, LANE - rope_blk, 1),
                                pltpu.roll(t, rope_blk, 1))
            o_ref[0, :, s * LANE:(s + 1) * LANE] = (t * cos + partner * sin).astype(o_ref.dtype)

    @pl.when(j >= n_rope)
    def _():
        o_ref[0] = acc.astype(o_ref.dtype)


def _proj(xa, mod, g, w, cos, sin, *, seq, n_rope, n_q, qscale, rope_blk, tm=1088, tn=512):
    b, t, d = xa.shape
    n = w.shape[1]
    kern = functools.partial(_proj_kernel, tm=tm, tn=tn, seq=seq, n_rope=n_rope, n_q=n_q,
                             qscale=qscale, rope_blk=rope_blk)
    return pl.pallas_call(
        kern,
        grid=(b, t // tm, n // tn),
        in_specs=[
            pl.BlockSpec((1, tm, d), lambda bi, i, j: (bi, i, 0)),
            pl.BlockSpec((1, 8, d), lambda bi, i, j: (bi, 0, 0)),
            pl.BlockSpec((1, d), lambda bi, i, j: (0, 0)),
            pl.BlockSpec((d, tn), lambda bi, i, j: (0, j)),
            pl.BlockSpec((tm, LANE), lambda bi, i, j: (i, 0)),
            pl.BlockSpec((tm, LANE), lambda bi, i, j: (i, 0)),
        ],
        out_specs=pl.BlockSpec((1, tm, tn), lambda bi, i, j: (bi, i, j)),
        out_shape=jax.ShapeDtypeStruct((b, t, n), BF16),
        scratch_shapes=[pltpu.VMEM((tm, d), BF16)],
        compiler_params=_params(("parallel", "parallel", "arbitrary"), 48),
        name="proj",
    )(xa, mod, g, w, cos, sin)


def _rope_tables(seq, n_ctx, head_dim):
    quarter = head_dim // 4
    inv = ROPE_BASE ** (-jnp.arange(quarter, dtype=F32) / quarter)
    pos = jnp.arange(seq)
    ar = (pos // GRID_W).astype(F32)[:, None] * inv[None, :]
    ac = (pos % GRID_W).astype(F32)[:, None] * inv[None, :]
    cos = jnp.concatenate([jnp.cos(ar), jnp.cos(ar), jnp.cos(ac), jnp.cos(ac)], axis=-1)
    sin = jnp.concatenate([-jnp.sin(ar), jnp.sin(ar), -jnp.sin(ac), jnp.sin(ac)], axis=-1)
    reps = LANE // head_dim
    cos = jnp.tile(cos, (1, reps))
    sin = jnp.tile(sin, (1, reps))
    cos = jnp.concatenate([cos, jnp.ones((n_ctx, LANE), F32)], axis=0)
    sin = jnp.concatenate([sin, jnp.zeros((n_ctx, LANE), F32)], axis=0)
    return cos, sin


def _diff_attn_kernel(lamp_ref, q_ref, k_ref, vt_ref, z_ref, sg_ref, o_ref,
                      s0_ref, s1_ref, acc_ref, *, tq, tk, first, n_pipe, lam_init):
    q = q_ref[0].astype(F32)
    lane = lax.broadcasted_iota(jnp.int32, q.shape, 1)
    qm = jnp.concatenate([jnp.where(lane < DA_HD, q, 0.0),
                          jnp.where(lane >= DA_HD, q, 0.0)], axis=0).astype(BF16)
    lp = lamp_ref[...]
    lam = (jnp.exp(jnp.sum(lp[0:1] * lp[1:2], axis=1, keepdims=True))
           - jnp.exp(jnp.sum(lp[2:3] * lp[3:4], axis=1, keepdims=True)) + lam_init)

    def scores(c0, cl):
        return lax.dot_general(k_ref[0, pl.ds(c0, cl), :], qm, (((1,), (1,)), ((), ())),
                               preferred_element_type=F32)

    def next_chunk(s, c0, cl, m, l):
        m_new = jnp.maximum(m, jnp.max(s, axis=0, keepdims=True))
        e = jnp.exp2(s - m_new)
        alpha = jnp.exp2(m - m_new)
        acc_ref[...] = alpha * acc_ref[...] + jnp.dot(
            vt_ref[0, 0, :, pl.ds(c0, cl)], e.astype(BF16), preferred_element_type=F32)
        return m_new, alpha * l + jnp.sum(e, axis=0, keepdims=True)

    if n_pipe:
        s0_ref[...] = scores(0, tk)
    s = scores(*first)
    m = jnp.max(s, axis=0, keepdims=True)
    e = jnp.exp2(s - m)
    l = jnp.sum(e, axis=0, keepdims=True)
    acc_ref[...] = jnp.dot(vt_ref[0, 0, :, pl.ds(*first)], e.astype(BF16),
                           preferred_element_type=F32)

    if n_pipe:
        def pair(p, carry):
            c0 = pl.multiple_of(p * 2 * tk, 2 * tk)
            s1_ref[...] = scores(c0 + tk, tk)
            m, l = next_chunk(s0_ref[...], c0, tk, *carry)
            s0_ref[...] = scores(c0 + 2 * tk, tk)
            return next_chunk(s1_ref[...], c0 + tk, tk, m, l)

        m, l = lax.fori_loop(0, n_pipe // 2 - 1, pair, (m, l))
        c0 = (n_pipe - 2) * tk
        s1_ref[...] = scores(c0 + tk, tk)
        m, l = next_chunk(s0_ref[...], c0, tk, m, l)
        m, l = next_chunk(s1_ref[...], c0 + tk, tk, m, l)

    inv = 1.0 / l
    acc = acc_ref[...]
    o_t = acc[:, :tq] * inv[:, :tq] - lam * (acc[:, tq:] * inv[:, tq:])
    o = o_t.T
    o = o * lax.rsqrt(jnp.mean(o * o, axis=-1, keepdims=True) + EPS) * sg_ref[...]
    o = o * (1.0 - lam_init)
    o_ref[0] = (o * _silu(z_ref[0].astype(F32))).astype(o_ref.dtype)


def _diff_attn(qkvz, vt, lamp, sg, *, seq, n_ctx, lam_init, ctx_queries, tq=512, tk=512):
    b, t, _ = qkvz.shape
    h = DA_HEADS
    if ctx_queries:
        tq, n_pipe, first = n_ctx, 0, (0, n_ctx)
        q_blk0, k_rows, k_blk = seq // n_ctx, n_ctx, seq // n_ctx
        out_rows = n_ctx
    else:
        n_pipe, first = seq // tk, (seq, n_ctx)
        q_blk0, k_rows, k_blk = 0, t, 0
        out_rows = seq
    kern = functools.partial(_diff_attn_kernel, tq=tq, tk=tk, first=first, n_pipe=n_pipe,
                             lam_init=lam_init)
    return pl.pallas_call(
        kern,
        grid=(b, h, out_rows // tq),
        in_specs=[
            pl.BlockSpec((4, DA_HD), lambda bi, hi, i: (0, 0)),
            pl.BlockSpec((1, tq, LANE), lambda bi, hi, i: (bi, q_blk0 + i, hi)),
            pl.BlockSpec((1, k_rows, LANE), lambda bi, hi, i: (bi, k_blk, h + hi)),
            pl.BlockSpec((1, 1, DA_VD, k_rows), lambda bi, hi, i: (bi, hi, 0, k_blk)),
            pl.BlockSpec((1, tq, LANE), lambda bi, hi, i: (bi, q_blk0 + i, 3 * h + hi)),
            pl.BlockSpec((1, DA_VD), lambda bi, hi, i: (0, 0)),
        ],
        out_specs=pl.BlockSpec((1, tq, LANE), lambda bi, hi, i: (bi, i, hi)),
        out_shape=jax.ShapeDtypeStruct((b, out_rows, h * DA_VD), BF16),
        scratch_shapes=[pltpu.VMEM((tk, 2 * tq), F32), pltpu.VMEM((tk, 2 * tq), F32),
                        pltpu.VMEM((DA_VD, 2 * tq), F32)],
        compiler_params=_params(("parallel", "parallel", "arbitrary"), 48),
        name="diff_attn_ctx" if ctx_queries else "diff_attn",
    )(lamp, qkvz, qkvz, vt, qkvz, sg)


def _swa_kernel(q_ref, k_ref, vt_ref, z_ref, sink_ref, o_ref, *, tq, seq, n_ctx, n_lat_tiles):
    i = pl.program_id(2)
    span = tq + 2 * WINDOW
    q4 = jnp.concatenate([q_ref[0, :, g * WC_HD:(g + 1) * WC_HD] for g in range(WC_G)],
                         axis=0)
    sink = sink_ref[0]
    nt = (((1,), (1,)), ((), ()))
    k_ctx = k_ref[0, seq:seq + n_ctx, :]
    vt_ctx = vt_ref[0, 0, :, seq:seq + n_ctx]
    s_ctx = lax.dot_general(k_ctx, q4, nt, preferred_element_type=F32)
    m_ctx = jnp.maximum(jnp.max(s_ctx, axis=0, keepdims=True), sink)

    def finish(o_t, den):
        o_t = o_t * (1.0 / den)
        for g in range(WC_G):
            o = o_t[:, g * tq:(g + 1) * tq].T
            z = z_ref[0, :, g * WC_HD:(g + 1) * WC_HD].astype(F32)
            o_ref[0, :, g * WC_HD:(g + 1) * WC_HD] = (o * _silu(z)).astype(o_ref.dtype)

    @pl.when(i < n_lat_tiles)
    def _():
        start = i * tq
        base = pl.multiple_of(jnp.clip(start - WINDOW, 0, seq - span), LANE)
        s_lat = lax.dot_general(k_ref[0, pl.ds(base, span), :], q4, nt,
                                preferred_element_type=F32)
        kpos = base + lax.broadcasted_iota(jnp.int32, (span, 1), 0)
        qpos = start + lax.rem(lax.broadcasted_iota(jnp.int32, (1, WC_G * tq), 1), tq)
        s_lat = jnp.where(jnp.abs(qpos - kpos) <= WINDOW, s_lat, -jnp.inf)
        m = jnp.maximum(m_ctx, jnp.max(s_lat, axis=0, keepdims=True))
        e_lat = jnp.exp2(s_lat - m)
        e_ctx = jnp.exp2(s_ctx - m)
        den = (jnp.sum(e_lat, axis=0, keepdims=True) + jnp.sum(e_ctx, axis=0, keepdims=True)
               + jnp.exp2(sink - m))
        o_t = (jnp.dot(vt_ref[0, 0, :, pl.ds(base, span)], e_lat.astype(BF16),
                       preferred_element_type=F32)
               + jnp.dot(vt_ctx, e_ctx.astype(BF16), preferred_element_type=F32))
        finish(o_t, den)

    @pl.when(i >= n_lat_tiles)
    def _():
        e_ctx = jnp.exp2(s_ctx - m_ctx)
        den = jnp.sum(e_ctx, axis=0, keepdims=True) + jnp.exp2(sink - m_ctx)
        finish(jnp.dot(vt_ctx, e_ctx.astype(BF16), preferred_element_type=F32), den)


def _swa_attn(qkvz, vt, sink_rows, *, seq, n_ctx, tq=256):
    b, t, _ = qkvz.shape
    gw = WC_G * WC_HD
    n_lat = seq // tq
    kern = functools.partial(_swa_kernel, tq=tq, seq=seq, n_ctx=n_ctx, n_lat_tiles=n_lat)
    k_blk0 = WC_KV * gw // WC_HD
    z_blk0 = (WC_KV * gw + 2 * WC_KV * WC_HD) // gw
    return pl.pallas_call(
        kern,
        grid=(b, WC_KV, t // tq),
        in_specs=[
            pl.BlockSpec((1, tq, gw), lambda bi, n, i: (bi, i, n)),
            pl.BlockSpec((1, t, WC_HD), lambda bi, n, i: (bi, 0, k_blk0 + n)),
            pl.BlockSpec((1, 1, WC_HD, t), lambda bi, n, i: (bi, n, 0, 0)),
            pl.BlockSpec((1, tq, gw), lambda bi, n, i: (bi, i, z_blk0 + n)),
            pl.BlockSpec((1, 1, WC_G * tq), lambda bi, n, i: (n, 0, 0)),
        ],
        out_specs=pl.BlockSpec((1, tq, gw), lambda bi, n, i: (bi, i, n)),
        out_shape=jax.ShapeDtypeStruct((b, t, WC_KV * gw), BF16),
        compiler_params=_params(("parallel", "parallel", "arbitrary"), 48),
        name="swa_attn",
    )(qkvz, qkvz, vt, qkvz, sink_rows)


HALO = 8


def _pool_kernel(x_ref, xp_ref, xn_ref, mod_ref, g_ref, w_ref, wg_ref, bg_ref, sc_ref,
                 o_ref, hb_ref, s2_ref, s4_ref, s8_ref, *, tm, seq, n_ctx, n_lat_tiles):
    i = pl.program_id(1)
    is_ctx = i >= n_lat_tiles
    shift = jnp.where(is_ctx, mod_ref[0, 3:4, :], mod_ref[0, 0:1, :])
    scale = jnp.where(is_ctx, mod_ref[0, 4:5, :], mod_ref[0, 1:2, :])
    g = g_ref[...]
    has_prev = jnp.logical_and(i != 0, i != n_lat_tiles)
    has_next = jnp.logical_and(i != n_lat_tiles - 1, i < n_lat_tiles)
    h = _norm_mod(x_ref[0], g, shift, scale)
    hb_ref[HALO:HALO + tm, :] = h
    hb_ref[0:HALO, :] = jnp.where(has_prev, _norm_mod(xp_ref[0], g, shift, scale), 0.0)
    hb_ref[HALO + tm:, :] = jnp.where(has_next, _norm_mod(xn_ref[0], g, shift, scale), 0.0)
    n = tm + 2 * HALO
    s2_ref[1:n, :] = hb_ref[0:n - 1, :] + hb_ref[1:n, :]
    s4_ref[2:n - 1, :] = s2_ref[1:n - 2, :] + s2_ref[3:n, :]
    s8_ref[4:n - 3, :] = s4_ref[2:n - 5, :] + s4_ref[6:n - 1, :]
    sums = (s2_ref[HALO:HALO + tm, :], s4_ref[HALO:HALO + tm, :], s8_ref[HALO:HALO + tm, :],
            s8_ref[HALO - 4:HALO - 4 + tm, :] + s8_ref[HALO + 4:HALO + 4 + tm, :])
    pos = jnp.where(is_ctx, 0, i * tm) + lax.broadcasted_iota(jnp.int32, (tm, 1), 0)
    seg_len = jnp.where(is_ctx, n_ctx, seq)
    hb16 = h.astype(BF16)
    d_inner = len(POOL_WINDOWS) * POOL_GW
    for gi, w in enumerate(POOL_WINDOWS):
        lo = jnp.maximum(pos - w // 2, 0)
        hi = jnp.minimum(pos - w // 2 + w, seg_len)
        cnt = (hi - lo).astype(F32)
        hd = (sums[gi] / cnt - h).astype(BF16)
        cols = slice(gi * POOL_GW, (gi + 1) * POOL_GW)
        d = jnp.dot(hd, w_ref[:, cols], preferred_element_type=F32)
        y = jnp.dot(d.astype(BF16), wg_ref[gi], preferred_element_type=F32) + bg_ref[:, cols]
        y = y * sc_ref[:, cols]
        z = jnp.dot(hb16, w_ref[:, d_inner + gi * POOL_GW:d_inner + (gi + 1) * POOL_GW],
                    preferred_element_type=F32)
        o_ref[0, :, cols] = (y * _silu(z)).astype(o_ref.dtype)


def _pool_mixer(xa, mod, g, w_in, w_grp, b_grp, scale, *, seq, n_ctx, tm=256):
    b, t, d = xa.shape
    d_inner = len(POOL_WINDOWS) * POOL_GW
    hb = tm // HALO
    last = t // HALO - 1
    kern = functools.partial(_pool_kernel, tm=tm, seq=seq, n_ctx=n_ctx, n_lat_tiles=seq // tm)
    buf = pltpu.VMEM((tm + 2 * HALO, d), F32)
    return pl.pallas_call(
        kern,
        grid=(b, t // tm),
        in_specs=[
            pl.BlockSpec((1, tm, d), lambda bi, i: (bi, i, 0)),
            pl.BlockSpec((1, HALO, d), lambda bi, i: (bi, jnp.maximum(i * hb - 1, 0), 0)),
            pl.BlockSpec((1, HALO, d), lambda bi, i: (bi, jnp.minimum((i + 1) * hb, last), 0)),
            pl.BlockSpec((1, 8, d), lambda bi, i: (bi, 0, 0)),
            pl.BlockSpec((1, d), lambda bi, i: (0, 0)),
            pl.BlockSpec((d, 2 * d_inner), lambda bi, i: (0, 0)),
            pl.BlockSpec((len(POOL_WINDOWS), POOL_GW, POOL_GW), lambda bi, i: (0, 0, 0)),
            pl.BlockSpec((1, d_inner), lambda bi, i: (0, 0)),
            pl.BlockSpec((1, d_inner), lambda bi, i: (0, 0)),
        ],
        out_specs=pl.BlockSpec((1, tm, d_inner), lambda bi, i: (bi, i, 0)),
        out_shape=jax.ShapeDtypeStruct((b, t, d_inner), BF16),
        scratch_shapes=[buf, buf, buf, buf],
        compiler_params=_params(("parallel", "arbitrary"), 56),
        name="pool_mixer",
    )(xa, xa, xa, mod, g, w_in, w_grp, b_grp, scale)


def _out_kernel(g_ref, gc_ref, w_ref, x_ref, mod_ref, fg_ref, o_ref, *, tm, seq, final, split):
    i = pl.program_id(1)
    is_ctx = i * tm >= seq

    def emit(g, gate):
        y = jnp.dot(g, w_ref[...], preferred_element_type=F32)
        xn = x_ref[0] + gate * y
        if final:
            xn = xn * lax.rsqrt(jnp.mean(xn * xn, axis=-1, keepdims=True) + EPS) * fg_ref[...]
        o_ref[0] = xn

    if split:
        @pl.when(jnp.logical_not(is_ctx))
        def _():
            emit(g_ref[0], mod_ref[0, 2:3, :])

        @pl.when(is_ctx)
        def _():
            emit(gc_ref[0], mod_ref[0, 5:6, :])
    else:
        row = i * tm + lax.broadcasted_iota(jnp.int32, (tm, 1), 0)
        emit(g_ref[0], jnp.where(row >= seq, mod_ref[0, 5:6, :], mod_ref[0, 2:3, :]))


def _out_proj(gact, gctx, w, xa, mod, final_g, *, seq, final, tm):
    b, _, d_inner = gact.shape
    d = w.shape[1]
    rows = seq if final else xa.shape[1]
    split = gctx is not None
    if split:
        assert seq % tm == 0 and gctx.shape[1] == tm
        last_lat = seq // tm - 1
        g_map = lambda bi, i: (bi, jnp.minimum(i, last_lat), 0)
    else:
        gctx = gact
        g_map = lambda bi, i: (bi, i, 0)
    kern = functools.partial(_out_kernel, tm=tm, seq=seq, final=final, split=split)
    return pl.pallas_call(
        kern,
        grid=(b, rows // tm),
        in_specs=[
            pl.BlockSpec((1, tm, d_inner), g_map),
            pl.BlockSpec((1, tm, d_inner), lambda bi, i: (bi, 0, 0)),
            pl.BlockSpec((d_inner, d), lambda bi, i: (0, 0)),
            pl.BlockSpec((1, tm, d), lambda bi, i: (bi, i, 0)),
            pl.BlockSpec((1, 8, d), lambda bi, i: (bi, 0, 0)),
            pl.BlockSpec((1, d), lambda bi, i: (0, 0)),
        ],
        out_specs=pl.BlockSpec((1, tm, d), lambda bi, i: (bi, i, 0)),
        out_shape=jax.ShapeDtypeStruct((b, rows, d), F32),
        compiler_params=_params(("parallel", "parallel"), 48),
        name="out_proj",
    )(gact, gctx, w, xa, mod, final_g)


def kernel(x, c, ctx, c_ctx, norm_g, w_ada, b_ada, a_w_in, a_w_out, a_lam_q1, a_lam_k1, a_lam_q2, a_lam_k2, a_subln_g, b_w_in, b_w_grp, b_b_grp, b_scale, b_w_out, c_w_in, c_sink, c_w_out, final_g):
    b, seq, d = x.shape
    n_ctx = ctx.shape[1]
    depth = w_ada.shape[0]
    xa = jnp.concatenate([x, ctx], axis=1)

    cond8 = jnp.concatenate([c, c_ctx[None, :], jnp.zeros((8 - b - 1, d), F32)], axis=0)
    ada = _adaln(cond8, w_ada, b_ada)

    cos_a, sin_a = _rope_tables(seq, n_ctx, DA_HD)
    cos_c, sin_c = _rope_tables(seq, n_ctx, WC_HD)

    out = None
    for i in range(depth):
        m = i % N_MIXERS
        j = i // N_MIXERS
        last = i == depth - 1
        lat = ada[i, :b].reshape(b, 3, d)
        cm = jnp.broadcast_to(ada[i, b].reshape(1, 3, d), (b, 3, d))
        mod = jnp.concatenate([lat, cm, jnp.zeros((b, 2, d), F32)], axis=1)
        g = norm_g[i].reshape(1, d)
        gctx = None

        if m == 0:
            lam_init = 0.8 - 0.6 * math.exp(-0.3 * i)
            qkvz = _proj(xa, mod, g, a_w_in[j].astype(BF16), cos_a, sin_a, seq=seq,
                         n_rope=8, n_q=4, qscale=DA_HD ** -0.5 * LOG2E, rope_blk=DA_HD // 4)
            v0 = 2 * DA_HEADS * DA_VD
            vt = qkvz[:, :, v0:v0 + DA_HEADS * DA_VD].reshape(b, -1, DA_HEADS, DA_VD)
            vt = vt.transpose(0, 2, 3, 1)
            lamp = jnp.stack([a_lam_q1[j], a_lam_k1[j], a_lam_q2[j], a_lam_k2[j]], axis=0)
            sg = a_subln_g[j].reshape(1, DA_VD)
            gact = _diff_attn(qkvz, vt, lamp, sg, seq=seq, n_ctx=n_ctx, lam_init=lam_init,
                              ctx_queries=False)
            if not last:
                gctx = _diff_attn(qkvz, vt, lamp, sg, seq=seq, n_ctx=n_ctx, lam_init=lam_init,
                                  ctx_queries=True)
            w_out = a_w_out[j]
        elif m == 1:
            gact = _pool_mixer(xa, mod, g, b_w_in[j].astype(BF16), b_w_grp[j].astype(BF16),
                               b_b_grp[j].reshape(1, -1), b_scale[j].reshape(1, -1),
                               seq=seq, n_ctx=n_ctx)
            w_out = b_w_out[j]
        else:
            qkvz = _proj(xa, mod, g, c_w_in[j].astype(BF16), cos_c, sin_c, seq=seq,
                         n_rope=5, n_q=4, qscale=WC_HD ** -0.5 * LOG2E, rope_blk=WC_HD // 4)
            v0 = WC_KV * WC_G * WC_HD + WC_KV * WC_HD
            vt = qkvz[:, :, v0:v0 + WC_KV * WC_HD].reshape(b, -1, WC_KV, WC_HD)
            vt = vt.transpose(0, 2, 3, 1)
            tq = 256
            sink_rows = jnp.repeat(c_sink[j].astype(F32) * LOG2E, tq).reshape(WC_KV, 1, WC_G * tq)
            gact = _swa_attn(qkvz, vt, sink_rows, seq=seq, n_ctx=n_ctx, tq=tq)
            w_out = c_w_out[j]

        if last:
            tm = 512
        else:
            tm = 544 if gctx is None else n_ctx
        res = _out_proj(gact, gctx, w_out.astype(BF16), xa, mod, final_g.reshape(1, d), seq=seq,
                        final=last, tm=tm)
        if last:
            out = res
        else:
            xa = res
    return out
```

```python
import functools
import math

import jax
import jax.numpy as jnp
from jax import lax
from jax.experimental import pallas as pl
from jax.experimental.pallas import tpu as pltpu

F32 = jnp.float32
BF16 = jnp.bfloat16

GRID_W = 64
EPS = 1e-6
ROPE_BASE = 10000.0
N_MIXERS = 3
DA_HEADS = 16
DA_HD = 64
DA_VD = 128
POOL_WINDOWS = (2, 4, 8, 16)
POOL_GW = 512
WC_HD = 128
WC_KV = 4
WC_G = 4
WINDOW = 128

LOG2E = 1.4426950408889634
LANE = 128
MIB = 1024 * 1024


def _params(semantics, vmem_mib):
    return pltpu.CompilerParams(dimension_semantics=semantics,
                                vmem_limit_bytes=vmem_mib * MIB)


def _silu(v):
    return v * jax.nn.sigmoid(v)


def _adaln_kernel(c_ref, w_ref, b_ref, o_ref):
    a = _silu(c_ref[...]).astype(BF16)
    o_ref[0] = jnp.dot(a, w_ref[0].astype(BF16), preferred_element_type=F32) + b_ref[0]


def _adaln(cond8, w_ada, b_ada):
    depth, d, d3 = w_ada.shape
    nt = d3 // d
    return pl.pallas_call(
        _adaln_kernel,
        grid=(depth, nt),
        in_specs=[
            pl.BlockSpec((8, d), lambda l, n: (0, 0)),
            pl.BlockSpec((1, d, d), lambda l, n: (l, 0, n)),
            pl.BlockSpec((1, 1, d), lambda l, n: (l, 0, n)),
        ],
        out_specs=pl.BlockSpec((1, 8, d), lambda l, n: (l, 0, n)),
        out_shape=jax.ShapeDtypeStruct((depth, 8, d3), F32),
        compiler_params=_params(("parallel", "parallel"), 32),
        name="adaln",
    )(cond8, w_ada, b_ada.reshape(depth, 1, d3))


def _norm_mod(xf, g, shift, scale):
    y = xf * lax.rsqrt(jnp.mean(xf * xf, axis=-1, keepdims=True) + EPS) * g
    return y * (1.0 + scale) + shift


SUB = 256
HALF = LANE // 2


def _rope_lane_layout(head_dim):
    quarter = head_dim // 4
    lane = jnp.arange(LANE)
    second = lane // HALF
    grp = (lane % HALF) // quarter
    freq = lane % quarter
    axis = grp % 2
    unit = grp // 2
    old = unit * head_dim + (axis * 2 + second) * quarter + freq
    sign = jnp.where(second == 0, -1.0, 1.0).astype(F32)
    return old, axis, freq, sign


def _rope_tables(seq, n_ctx, head_dim):
    quarter = head_dim // 4
    _, axis, freq, sign = _rope_lane_layout(head_dim)
    inv = ROPE_BASE ** (-jnp.arange(quarter, dtype=F32) / quarter)
    pos = jnp.arange(seq)
    coord = jnp.where(axis[None, :] == 0, (pos // GRID_W)[:, None], (pos % GRID_W)[:, None])
    ang = coord.astype(F32) * inv[freq][None, :]
    cos = jnp.concatenate([jnp.cos(ang), jnp.ones((n_ctx, LANE), F32)], axis=0)
    sin = jnp.concatenate([jnp.sin(ang) * sign[None, :], jnp.zeros((n_ctx, LANE), F32)], axis=0)
    return cos, sin


def _permute_rope_columns(w, n_cols, head_dim):
    old, _, _, _ = _rope_lane_layout(head_dim)
    idx = (jnp.arange(n_cols) // LANE) * LANE + jnp.tile(old, n_cols // LANE)
    return jnp.concatenate([w[:, idx], w[:, n_cols:]], axis=1)


def _proj_kernel(x_ref, mod_ref, g_ref, w_ref, cos_ref, sin_ref, o_ref, h_ref,
                 *, tm, seq, groups, qscale):
    i = pl.program_id(1)
    j = pl.program_id(2)

    @pl.when(j == 0)
    def _():
        row = i * tm + lax.broadcasted_iota(jnp.int32, (tm, 1), 0)
        is_ctx = row >= seq
        shift = jnp.where(is_ctx, mod_ref[0, 3:4, :], mod_ref[0, 0:1, :])
        scale = jnp.where(is_ctx, mod_ref[0, 4:5, :], mod_ref[0, 1:2, :])
        h_ref[...] = _norm_mod(x_ref[0], g_ref[...], shift, scale).astype(BF16)

    def emit(kinds):
        tables = {}
        for kind in set(kinds) - {"P"}:
            mul = qscale if kind == "Q" else 1.0
            tables[kind] = (cos_ref[...] * mul, sin_ref[...] * mul)
        for s, kind in enumerate(kinds):
            acc = jnp.dot(h_ref[...], w_ref[:, s * SUB:(s + 1) * SUB],
                          preferred_element_type=F32)
            if kind == "P":
                o_ref[0, :, s * SUB:(s + 1) * SUB] = acc.astype(o_ref.dtype)
                continue
            cos, sin = tables[kind]
            for u in range(SUB // LANE):
                t = acc[:, u * LANE:(u + 1) * LANE]
                c0 = s * SUB + u * LANE
                o_ref[0, :, c0:c0 + LANE] = (t * cos + pltpu.roll(t, HALF, 1) * sin
                                             ).astype(o_ref.dtype)

    for j_lo, j_hi, kinds in groups:
        pl.when(jnp.logical_and(j >= j_lo, j <= j_hi))(functools.partial(emit, kinds))


def _proj(xa, mod, g, w, cos, sin, *, seq, n_q, n_k, qscale, tm=1088, tn=1024):
    b, t, d = xa.shape
    n = w.shape[1]
    per_tile = tn // SUB
    kinds = ["Q"] * (n_q // SUB) + ["K"] * (n_k // SUB) + ["P"] * ((n - n_q - n_k) // SUB)
    tiles = [tuple(kinds[jt * per_tile:(jt + 1) * per_tile]) for jt in range(n // tn)]
    groups = []
    for jt, kt in enumerate(tiles):
        if groups and groups[-1][2] == kt:
            groups[-1] = (groups[-1][0], jt, kt)
        else:
            groups.append((jt, jt, kt))
    kern = functools.partial(_proj_kernel, tm=tm, seq=seq, groups=tuple(groups), qscale=qscale)
    return pl.pallas_call(
        kern,
        grid=(b, t // tm, n // tn),
        in_specs=[
            pl.BlockSpec((1, tm, d), lambda bi, i, j: (bi, i, 0)),
            pl.BlockSpec((1, 8, d), lambda bi, i, j: (bi, 0, 0)),
            pl.BlockSpec((1, d), lambda bi, i, j: (0, 0)),
            pl.BlockSpec((d, tn), lambda bi, i, j: (0, j)),
            pl.BlockSpec((tm, LANE), lambda bi, i, j: (i, 0)),
            pl.BlockSpec((tm, LANE), lambda bi, i, j: (i, 0)),
        ],
        out_specs=pl.BlockSpec((1, tm, tn), lambda bi, i, j: (bi, i, j)),
        out_shape=jax.ShapeDtypeStruct((b, t, n), BF16),
        scratch_shapes=[pltpu.VMEM((tm, d), BF16)],
        compiler_params=_params(("parallel", "parallel", "arbitrary"), 48),
        name="proj",
    )(xa, mod, g, w, cos, sin)


JUMP_LIMIT = 64.0


def _diff_attn_kernel(lamp_ref, q_ref, k_ref, vt_ref, z_ref, sg_ref, o_ref, acc_ref,
                      *, tq, tk, chunks, lam_init):
    q = q_ref[0].astype(F32)
    lane = lax.broadcasted_iota(jnp.int32, q.shape, 1)
    map0 = lane % HALF < HALF // 2
    qm = jnp.concatenate([jnp.where(map0, q, 0.0),
                          jnp.where(map0, 0.0, q)], axis=0).astype(BF16)
    lp = lamp_ref[...]
    lam = (jnp.exp(jnp.sum(lp[0:1] * lp[1:2], axis=1, keepdims=True))
           - jnp.exp(jnp.sum(lp[2:3] * lp[3:4], axis=1, keepdims=True)) + lam_init)

    def scores(c0, cl):
        return lax.dot_general(k_ref[0, pl.ds(c0, cl), :], qm, (((1,), (1,)), ((), ())),
                               preferred_element_type=F32)

    def pv(e, c0, cl):
        return jnp.dot(vt_ref[0, 0, :, pl.ds(c0, cl)], e.astype(BF16),
                       preferred_element_type=F32)

    def finish(l):
        inv = 1.0 / l
        acc = acc_ref[...]
        o_t = acc[:, :tq] * inv[:, :tq] - lam * (acc[:, tq:] * inv[:, tq:])
        o = o_t.T
        o = o * lax.rsqrt(jnp.mean(o * o, axis=-1, keepdims=True) + EPS) * sg_ref[...]
        o = o * (1.0 - lam_init)
        o_ref[0] = (o * _silu(z_ref[0].astype(F32))).astype(o_ref.dtype)

    m = l = jump = None
    for c0, cl in chunks:
        s = scores(c0, cl)
        if m is None:
            m = jnp.max(s[0:8], axis=0, keepdims=True)
        e = jnp.exp2(s - m)
        mc = jnp.max(s, axis=0, keepdims=True)
        m_new = jnp.maximum(m, mc)
        alpha = jnp.exp2(m - m_new)
        if l is None:
            acc_ref[...] = pv(e, c0, cl) * alpha
            l = jnp.sum(e, axis=0, keepdims=True) * alpha
            jump = mc - m
        else:
            acc_ref[...] = (acc_ref[...] + pv(e, c0, cl)) * alpha
            l = (l + jnp.sum(e, axis=0, keepdims=True)) * alpha
            jump = jnp.maximum(jump, mc - m)
        m = m_new
    finish(l)

    @pl.when(jnp.max(jump) > JUMP_LIMIT)
    def _():
        s = scores(*chunks[0])
        m = jnp.max(s, axis=0, keepdims=True)
        e = jnp.exp2(s - m)
        acc_ref[...] = pv(e, *chunks[0])
        l = jnp.sum(e, axis=0, keepdims=True)

        def body(c, carry):
            m, l = carry
            c0 = pl.multiple_of(chunks[1][0] + c * tk, tk)
            s = scores(c0, tk)
            m_new = jnp.maximum(m, jnp.max(s, axis=0, keepdims=True))
            e = jnp.exp2(s - m_new)
            alpha = jnp.exp2(m - m_new)
            acc_ref[...] = alpha * acc_ref[...] + pv(e, c0, tk)
            return m_new, alpha * l + jnp.sum(e, axis=0, keepdims=True)

        if len(chunks) > 1:
            m, l = lax.fori_loop(0, len(chunks) - 1, body, (m, l))
        finish(l)


def _diff_attn(qkvz, vt, lamp, sg, *, seq, n_ctx, lam_init, ctx_queries, tq=512, tk=512):
    b, t, _ = qkvz.shape
    h = DA_HEADS
    if ctx_queries:
        tq, chunks = n_ctx, ((0, n_ctx),)
        q_blk0, k_rows, k_blk = seq // n_ctx, n_ctx, seq // n_ctx
        out_rows = n_ctx
    else:
        chunks = ((seq, n_ctx),) + tuple((c, tk) for c in range(0, seq, tk))
        q_blk0, k_rows, k_blk = 0, t, 0
        out_rows = seq
    kern = functools.partial(_diff_attn_kernel, tq=tq, tk=tk, chunks=chunks, lam_init=lam_init)
    return pl.pallas_call(
        kern,
        grid=(b, h, out_rows // tq),
        in_specs=[
            pl.BlockSpec((4, DA_HD), lambda bi, hi, i: (0, 0)),
            pl.BlockSpec((1, tq, LANE), lambda bi, hi, i: (bi, q_blk0 + i, hi)),
            pl.BlockSpec((1, k_rows, LANE), lambda bi, hi, i: (bi, k_blk, h + hi)),
            pl.BlockSpec((1, 1, DA_VD, k_rows), lambda bi, hi, i: (bi, hi, 0, k_blk)),
            pl.BlockSpec((1, tq, LANE), lambda bi, hi, i: (bi, q_blk0 + i, 3 * h + hi)),
            pl.BlockSpec((1, DA_VD), lambda bi, hi, i: (0, 0)),
        ],
        out_specs=pl.BlockSpec((1, tq, LANE), lambda bi, hi, i: (bi, i, hi)),
        out_shape=jax.ShapeDtypeStruct((b, out_rows, h * DA_VD), BF16),
        scratch_shapes=[pltpu.VMEM((DA_VD, 2 * tq), F32)],
        compiler_params=_params(("parallel", "parallel", "arbitrary"), 48),
        name="diff_attn_ctx" if ctx_queries else "diff_attn",
    )(lamp, qkvz, qkvz, vt, qkvz, sg)


def _swa_kernel(q_ref, k_ref, vt_ref, z_ref, sink_ref, o_ref, *, tq, seq, n_ctx, n_lat_tiles):
    i = pl.program_id(2)
    span = tq + 2 * WINDOW
    q4 = jnp.concatenate([q_ref[0, :, g * WC_HD:(g + 1) * WC_HD] for g in range(WC_G)],
                         axis=0)
    sink = sink_ref[0]
    nt = (((1,), (1,)), ((), ()))
    k_ctx = k_ref[0, seq:seq + n_ctx, :]
    vt_ctx = vt_ref[0, 0, :, seq:seq + n_ctx]
    s_ctx = lax.dot_general(k_ctx, q4, nt, preferred_element_type=F32)
    m_ctx = jnp.maximum(jnp.max(s_ctx, axis=0, keepdims=True), sink)

    def finish(o_t, den):
        o_t = o_t * (1.0 / den)
        for g in range(WC_G):
            o = o_t[:, g * tq:(g + 1) * tq].T
            z = z_ref[0, :, g * WC_HD:(g + 1) * WC_HD].astype(F32)
            o_ref[0, :, g * WC_HD:(g + 1) * WC_HD] = (o * _silu(z)).astype(o_ref.dtype)

    @pl.when(i < n_lat_tiles)
    def _():
        start = i * tq
        base = pl.multiple_of(jnp.clip(start - WINDOW, 0, seq - span), LANE)
        s_lat = lax.dot_general(k_ref[0, pl.ds(base, span), :], q4, nt,
                                preferred_element_type=F32)
        kpos = base + lax.broadcasted_iota(jnp.int32, (span, 1), 0)
        qpos = start + lax.broadcasted_iota(jnp.int32, (1, tq), 1)
        bias = jnp.where(jnp.abs(qpos - kpos) <= WINDOW, 0.0, -jnp.inf)
        s_lat = s_lat + jnp.concatenate([bias] * WC_G, axis=1)
        m = jnp.maximum(m_ctx, jnp.max(s_lat, axis=0, keepdims=True))
        e_lat = jnp.exp2(s_lat - m)
        e_ctx = jnp.exp2(s_ctx - m)
        den = (jnp.sum(e_lat, axis=0, keepdims=True) + jnp.sum(e_ctx, axis=0, keepdims=True)
               + jnp.exp2(sink - m))
        o_t = (jnp.dot(vt_ref[0, 0, :, pl.ds(base, span)], e_lat.astype(BF16),
                       preferred_element_type=F32)
               + jnp.dot(vt_ctx, e_ctx.astype(BF16), preferred_element_type=F32))
        finish(o_t, den)

    @pl.when(i >= n_lat_tiles)
    def _():
        e_ctx = jnp.exp2(s_ctx - m_ctx)
        den = jnp.sum(e_ctx, axis=0, keepdims=True) + jnp.exp2(sink - m_ctx)
        finish(jnp.dot(vt_ctx, e_ctx.astype(BF16), preferred_element_type=F32), den)


def _swa_attn(qkvz, vt, sink_rows, *, seq, n_ctx, tq=256):
    b, t, _ = qkvz.shape
    gw = WC_G * WC_HD
    n_lat = seq // tq
    kern = functools.partial(_swa_kernel, tq=tq, seq=seq, n_ctx=n_ctx, n_lat_tiles=n_lat)
    k_blk0 = WC_KV * gw // WC_HD
    z_blk0 = (WC_KV * gw + 2 * WC_KV * WC_HD) // gw
    return pl.pallas_call(
        kern,
        grid=(b, WC_KV, t // tq),
        in_specs=[
            pl.BlockSpec((1, tq, gw), lambda bi, n, i: (bi, i, n)),
            pl.BlockSpec((1, t, WC_HD), lambda bi, n, i: (bi, 0, k_blk0 + n)),
            pl.BlockSpec((1, 1, WC_HD, t), lambda bi, n, i: (bi, n, 0, 0)),
            pl.BlockSpec((1, tq, gw), lambda bi, n, i: (bi, i, z_blk0 + n)),
            pl.BlockSpec((1, 1, WC_G * tq), lambda bi, n, i: (n, 0, 0)),
        ],
        out_specs=pl.BlockSpec((1, tq, gw), lambda bi, n, i: (bi, i, n)),
        out_shape=jax.ShapeDtypeStruct((b, t, WC_KV * gw), BF16),
        compiler_params=_params(("parallel", "parallel", "arbitrary"), 48),
        name="swa_attn",
    )(qkvz, qkvz, vt, qkvz, sink_rows)


HALO = 8


def _pool_kernel(x_ref, xp_ref, xn_ref, mod_ref, g_ref, w_ref, wg_ref, bg_ref, sc_ref,
                 o_ref, hb_ref, s2_ref, s4_ref, s8_ref, *, tm, seq, n_ctx, n_lat_tiles):
    i = pl.program_id(1)
    is_ctx = i >= n_lat_tiles
    shift = jnp.where(is_ctx, mod_ref[0, 3:4, :], mod_ref[0, 0:1, :])
    scale = jnp.where(is_ctx, mod_ref[0, 4:5, :], mod_ref[0, 1:2, :])
    g = g_ref[...]
    has_prev = jnp.logical_and(i != 0, i != n_lat_tiles)
    has_next = jnp.logical_and(i != n_lat_tiles - 1, i < n_lat_tiles)
    h = _norm_mod(x_ref[0], g, shift, scale)
    hb_ref[HALO:HALO + tm, :] = h
    hb_ref[0:HALO, :] = jnp.where(has_prev, _norm_mod(xp_ref[0], g, shift, scale), 0.0)
    hb_ref[HALO + tm:, :] = jnp.where(has_next, _norm_mod(xn_ref[0], g, shift, scale), 0.0)
    n = tm + 2 * HALO
    s2_ref[1:n, :] = hb_ref[0:n - 1, :] + hb_ref[1:n, :]
    s4_ref[2:n - 1, :] = s2_ref[1:n - 2, :] + s2_ref[3:n, :]
    s8_ref[4:n - 3, :] = s4_ref[2:n - 5, :] + s4_ref[6:n - 1, :]
    sums = (s2_ref[HALO:HALO + tm, :], s4_ref[HALO:HALO + tm, :], s8_ref[HALO:HALO + tm, :],
            s8_ref[HALO - 4:HALO - 4 + tm, :] + s8_ref[HALO + 4:HALO + 4 + tm, :])
    pos = jnp.where(is_ctx, 0, i * tm) + lax.broadcasted_iota(jnp.int32, (tm, 1), 0)
    seg_len = jnp.where(is_ctx, n_ctx, seq)
    hb16 = h.astype(BF16)
    d_inner = len(POOL_WINDOWS) * POOL_GW
    for gi, w in enumerate(POOL_WINDOWS):
        lo = jnp.maximum(pos - w // 2, 0)
        hi = jnp.minimum(pos - w // 2 + w, seg_len)
        cnt = (hi - lo).astype(F32)
        hd = (sums[gi] / cnt - h).astype(BF16)
        cols = slice(gi * POOL_GW, (gi + 1) * POOL_GW)
        d = jnp.dot(hd, w_ref[:, cols], preferred_element_type=F32)
        y = jnp.dot(d.astype(BF16), wg_ref[gi], preferred_element_type=F32) + bg_ref[:, cols]
        y = y * sc_ref[:, cols]
        z = jnp.dot(hb16, w_ref[:, d_inner + gi * POOL_GW:d_inner + (gi + 1) * POOL_GW],
                    preferred_element_type=F32)
        o_ref[0, :, cols] = (y * _silu(z)).astype(o_ref.dtype)


def _pool_mixer(xa, mod, g, w_in, w_grp, b_grp, scale, *, seq, n_ctx, tm=256):
    b, t, d = xa.shape
    d_inner = len(POOL_WINDOWS) * POOL_GW
    hb = tm // HALO
    last = t // HALO - 1
    kern = functools.partial(_pool_kernel, tm=tm, seq=seq, n_ctx=n_ctx, n_lat_tiles=seq // tm)
    buf = pltpu.VMEM((tm + 2 * HALO, d), F32)
    return pl.pallas_call(
        kern,
        grid=(b, t // tm),
        in_specs=[
            pl.BlockSpec((1, tm, d), lambda bi, i: (bi, i, 0)),
            pl.BlockSpec((1, HALO, d), lambda bi, i: (bi, jnp.maximum(i * hb - 1, 0), 0)),
            pl.BlockSpec((1, HALO, d), lambda bi, i: (bi, jnp.minimum((i + 1) * hb, last), 0)),
            pl.BlockSpec((1, 8, d), lambda bi, i: (bi, 0, 0)),
            pl.BlockSpec((1, d), lambda bi, i: (0, 0)),
            pl.BlockSpec((d, 2 * d_inner), lambda bi, i: (0, 0)),
            pl.BlockSpec((len(POOL_WINDOWS), POOL_GW, POOL_GW), lambda bi, i: (0, 0, 0)),
            pl.BlockSpec((1, d_inner), lambda bi, i: (0, 0)),
            pl.BlockSpec((1, d_inner), lambda bi, i: (0, 0)),
        ],
        out_specs=pl.BlockSpec((1, tm, d_inner), lambda bi, i: (bi, i, 0)),
        out_shape=jax.ShapeDtypeStruct((b, t, d_inner), BF16),
        scratch_shapes=[buf, buf, buf, buf],
        compiler_params=_params(("parallel", "arbitrary"), 56),
        name="pool_mixer",
    )(xa, xa, xa, mod, g, w_in, w_grp, b_grp, scale)


def _out_kernel(g_ref, gc_ref, w_ref, x_ref, mod_ref, fg_ref, o_ref, *, tm, seq, final, split):
    i = pl.program_id(1)
    is_ctx = i * tm >= seq

    def emit(g, gate):
        y = jnp.dot(g, w_ref[...], preferred_element_type=F32)
        xn = x_ref[0] + gate * y
        if final:
            xn = xn * lax.rsqrt(jnp.mean(xn * xn, axis=-1, keepdims=True) + EPS) * fg_ref[...]
        o_ref[0] = xn

    if split:
        @pl.when(jnp.logical_not(is_ctx))
        def _():
            emit(g_ref[0], mod_ref[0, 2:3, :])

        @pl.when(is_ctx)
        def _():
            emit(gc_ref[0], mod_ref[0, 5:6, :])
    else:
        row = i * tm + lax.broadcasted_iota(jnp.int32, (tm, 1), 0)
        emit(g_ref[0], jnp.where(row >= seq, mod_ref[0, 5:6, :], mod_ref[0, 2:3, :]))


def _out_proj(gact, gctx, w, xa, mod, final_g, *, seq, final, tm):
    b, _, d_inner = gact.shape
    d = w.shape[1]
    rows = seq if final else xa.shape[1]
    split = gctx is not None
    if split:
        assert seq % tm == 0 and gctx.shape[1] == tm
        last_lat = seq // tm - 1
        g_map = lambda bi, i: (bi, jnp.minimum(i, last_lat), 0)
    else:
        gctx = gact
        g_map = lambda bi, i: (bi, i, 0)
    kern = functools.partial(_out_kernel, tm=tm, seq=seq, final=final, split=split)
    return pl.pallas_call(
        kern,
        grid=(b, rows // tm),
        in_specs=[
            pl.BlockSpec((1, tm, d_inner), g_map),
            pl.BlockSpec((1, tm, d_inner), lambda bi, i: (bi, 0, 0)),
            pl.BlockSpec((d_inner, d), lambda bi, i: (0, 0)),
            pl.BlockSpec((1, tm, d), lambda bi, i: (bi, i, 0)),
            pl.BlockSpec((1, 8, d), lambda bi, i: (bi, 0, 0)),
            pl.BlockSpec((1, d), lambda bi, i: (0, 0)),
        ],
        out_specs=pl.BlockSpec((1, tm, d), lambda bi, i: (bi, i, 0)),
        out_shape=jax.ShapeDtypeStruct((b, rows, d), F32),
        compiler_params=_params(("parallel", "parallel"), 48),
        name="out_proj",
    )(gact, gctx, w, xa, mod, final_g)


def kernel(x, c, ctx, c_ctx, norm_g, w_ada, b_ada, a_w_in, a_w_out, a_lam_q1, a_lam_k1, a_lam_q2, a_lam_k2, a_subln_g, b_w_in, b_w_grp, b_b_grp, b_scale, b_w_out, c_w_in, c_sink, c_w_out, final_g):
    b, seq, d = x.shape
    n_ctx = ctx.shape[1]
    depth = w_ada.shape[0]
    xa = jnp.concatenate([x, ctx], axis=1)

    cond8 = jnp.concatenate([c, c_ctx[None, :], jnp.zeros((8 - b - 1, d), F32)], axis=0)
    ada = _adaln(cond8, w_ada, b_ada)

    cos_a, sin_a = _rope_tables(seq, n_ctx, DA_HD)
    cos_c, sin_c = _rope_tables(seq, n_ctx, WC_HD)

    out = None
    for i in range(depth):
        m = i % N_MIXERS
        j = i // N_MIXERS
        last = i == depth - 1
        lat = ada[i, :b].reshape(b, 3, d)
        cm = jnp.broadcast_to(ada[i, b].reshape(1, 3, d), (b, 3, d))
        mod = jnp.concatenate([lat, cm, jnp.zeros((b, 2, d), F32)], axis=1)
        g = norm_g[i].reshape(1, d)
        gctx = None

        if m == 0:
            lam_init = 0.8 - 0.6 * math.exp(-0.3 * i)
            n_qk = DA_HEADS * 2 * DA_HD
            w_in = _permute_rope_columns(a_w_in[j], 2 * n_qk, DA_HD).astype(BF16)
            qkvz = _proj(xa, mod, g, w_in, cos_a, sin_a, seq=seq, n_q=n_qk, n_k=n_qk,
                         qscale=DA_HD ** -0.5 * LOG2E)
            v0 = 2 * DA_HEADS * DA_VD
            vt = qkvz[:, :, v0:v0 + DA_HEADS * DA_VD].reshape(b, -1, DA_HEADS, DA_VD)
            vt = vt.transpose(0, 2, 3, 1)
            lamp = jnp.stack([a_lam_q1[j], a_lam_k1[j], a_lam_q2[j], a_lam_k2[j]], axis=0)
            sg = a_subln_g[j].reshape(1, DA_VD)
            gact = _diff_attn(qkvz, vt, lamp, sg, seq=seq, n_ctx=n_ctx, lam_init=lam_init,
                              ctx_queries=False)
            if not last:
                gctx = _diff_attn(qkvz, vt, lamp, sg, seq=seq, n_ctx=n_ctx, lam_init=lam_init,
                                  ctx_queries=True)
            w_out = a_w_out[j]
        elif m == 1:
            gact = _pool_mixer(xa, mod, g, b_w_in[j].astype(BF16), b_w_grp[j].astype(BF16),
                               b_b_grp[j].reshape(1, -1), b_scale[j].reshape(1, -1),
                               seq=seq, n_ctx=n_ctx)
            w_out = b_w_out[j]
        else:
            n_q, n_k = WC_KV * WC_G * WC_HD, WC_KV * WC_HD
            w_in = _permute_rope_columns(c_w_in[j], n_q + n_k, WC_HD).astype(BF16)
            qkvz = _proj(xa, mod, g, w_in, cos_c, sin_c, seq=seq, n_q=n_q, n_k=n_k,
                         qscale=WC_HD ** -0.5 * LOG2E)
            v0 = WC_KV * WC_G * WC_HD + WC_KV * WC_HD
            vt = qkvz[:, :, v0:v0 + WC_KV * WC_HD].reshape(b, -1, WC_KV, WC_HD)
            vt = vt.transpose(0, 2, 3, 1)
            tq = 256
            sink_rows = jnp.repeat(c_sink[j].astype(F32) * LOG2E, tq).reshape(WC_KV, 1, WC_G * tq)
            gact = _swa_attn(qkvz, vt, sink_rows, seq=seq, n_ctx=n_ctx, tq=tq)
            w_out = c_w_out[j]

        if last:
            tm = 512
        else:
            tm = 544 if gctx is None else n_ctx
        res = _out_proj(gact, gctx, w_out.astype(BF16), xa, mod, final_g.reshape(1, d), seq=seq,
                        final=last, tm=tm)
        if last:
            out = res
        else:
            xa = res
    return out
```

```python
import functools
import math

import jax
import jax.numpy as jnp
from jax import lax
from jax.experimental import pallas as pl
from jax.experimental.pallas import tpu as pltpu

F32 = jnp.float32
BF16 = jnp.bfloat16

GRID_W = 64
EPS = 1e-6
ROPE_BASE = 10000.0
N_MIXERS = 3
DA_HEADS = 16
DA_HD = 64
DA_VD = 128
POOL_WINDOWS = (2, 4, 8, 16)
POOL_GW = 512
WC_HD = 128
WC_KV = 4
WC_G = 4
WINDOW = 128

LOG2E = 1.4426950408889634
LANE = 128
MIB = 1024 * 1024


def _params(semantics, vmem_mib):
    return pltpu.CompilerParams(dimension_semantics=semantics,
                                vmem_limit_bytes=vmem_mib * MIB)


def _silu(v):
    return v * jax.nn.sigmoid(v)


def _adaln_kernel(c_ref, w_ref, b_ref, o_ref):
    a = _silu(c_ref[...]).astype(BF16)
    o_ref[0] = jnp.dot(a, w_ref[0].astype(BF16), preferred_element_type=F32) + b_ref[0]


def _adaln(cond8, w_ada, b_ada):
    depth, d, d3 = w_ada.shape
    nt = d3 // d
    return pl.pallas_call(
        _adaln_kernel,
        grid=(depth, nt),
        in_specs=[
            pl.BlockSpec((8, d), lambda l, n: (0, 0)),
            pl.BlockSpec((1, d, d), lambda l, n: (l, 0, n)),
            pl.BlockSpec((1, 1, d), lambda l, n: (l, 0, n)),
        ],
        out_specs=pl.BlockSpec((1, 8, d), lambda l, n: (l, 0, n)),
        out_shape=jax.ShapeDtypeStruct((depth, 8, d3), F32),
        compiler_params=_params(("parallel", "parallel"), 32),
        name="adaln",
    )(cond8, w_ada, b_ada.reshape(depth, 1, d3))


def _norm_mod(xf, g, shift, scale):
    y = xf * lax.rsqrt(jnp.mean(xf * xf, axis=-1, keepdims=True) + EPS) * g
    return y * (1.0 + scale) + shift


SUB = 256
HALF = LANE // 2


def _rope_lane_layout(head_dim):
    quarter = head_dim // 4
    lane = jnp.arange(LANE)
    second = lane // HALF
    grp = (lane % HALF) // quarter
    freq = lane % quarter
    axis = grp % 2
    unit = grp // 2
    old = unit * head_dim + (axis * 2 + second) * quarter + freq
    sign = jnp.where(second == 0, -1.0, 1.0).astype(F32)
    return old, axis, freq, sign


def _rope_tables(seq, n_ctx, head_dim):
    quarter = head_dim // 4
    _, axis, freq, sign = _rope_lane_layout(head_dim)
    inv = ROPE_BASE ** (-jnp.arange(quarter, dtype=F32) / quarter)
    pos = jnp.arange(seq)
    coord = jnp.where(axis[None, :] == 0, (pos // GRID_W)[:, None], (pos % GRID_W)[:, None])
    ang = coord.astype(F32) * inv[freq][None, :]
    cos = jnp.concatenate([jnp.cos(ang), jnp.ones((n_ctx, LANE), F32)], axis=0)
    sin = jnp.concatenate([jnp.sin(ang) * sign[None, :], jnp.zeros((n_ctx, LANE), F32)], axis=0)
    return cos, sin


def _permute_rope_columns(w, n_cols, head_dim):
    quarter = head_dim // 4
    units = LANE // head_dim
    d = w.shape[0]
    head = w[:, :n_cols].reshape(d, n_cols // LANE, units, 2, 2, quarter)
    head = head.transpose(0, 1, 4, 2, 3, 5).reshape(d, n_cols)
    return jnp.concatenate([head, w[:, n_cols:]], axis=1)


def _proj_kernel(x_ref, mod_ref, g_ref, w_ref, cos_ref, sin_ref, o_ref, h_ref,
                 *, tm, seq, groups, qscale):
    i = pl.program_id(1)
    j = pl.program_id(2)

    @pl.when(j == 0)
    def _():
        row = i * tm + lax.broadcasted_iota(jnp.int32, (tm, 1), 0)
        is_ctx = row >= seq
        shift = jnp.where(is_ctx, mod_ref[0, 3:4, :], mod_ref[0, 0:1, :])
        scale = jnp.where(is_ctx, mod_ref[0, 4:5, :], mod_ref[0, 1:2, :])
        h_ref[...] = _norm_mod(x_ref[0], g_ref[...], shift, scale).astype(BF16)

    def emit(kinds):
        tables = {}
        for kind in set(kinds) - {"P"}:
            mul = qscale if kind == "Q" else 1.0
            tables[kind] = (cos_ref[...] * mul, sin_ref[...] * mul)
        for s, kind in enumerate(kinds):
            acc = jnp.dot(h_ref[...], w_ref[:, s * SUB:(s + 1) * SUB],
                          preferred_element_type=F32)
            if kind == "P":
                o_ref[0, :, s * SUB:(s + 1) * SUB] = acc.astype(o_ref.dtype)
                continue
            cos, sin = tables[kind]
            for u in range(SUB // LANE):
                t = acc[:, u * LANE:(u + 1) * LANE]
                c0 = s * SUB + u * LANE
                o_ref[0, :, c0:c0 + LANE] = (t * cos + pltpu.roll(t, HALF, 1) * sin
                                             ).astype(o_ref.dtype)

    for j_lo, j_hi, kinds in groups:
        pl.when(jnp.logical_and(j >= j_lo, j <= j_hi))(functools.partial(emit, kinds))


def _proj(xa, mod, g, w, cos, sin, *, seq, n_q, n_k, qscale, tm=1088, tn=1024):
    b, t, d = xa.shape
    n = w.shape[1]
    per_tile = tn // SUB
    kinds = ["Q"] * (n_q // SUB) + ["K"] * (n_k // SUB) + ["P"] * ((n - n_q - n_k) // SUB)
    tiles = [tuple(kinds[jt * per_tile:(jt + 1) * per_tile]) for jt in range(n // tn)]
    groups = []
    for jt, kt in enumerate(tiles):
        if groups and groups[-1][2] == kt:
            groups[-1] = (groups[-1][0], jt, kt)
        else:
            groups.append((jt, jt, kt))
    kern = functools.partial(_proj_kernel, tm=tm, seq=seq, groups=tuple(groups), qscale=qscale)
    return pl.pallas_call(
        kern,
        grid=(b, t // tm, n // tn),
        in_specs=[
            pl.BlockSpec((1, tm, d), lambda bi, i, j: (bi, i, 0)),
            pl.BlockSpec((1, 8, d), lambda bi, i, j: (bi, 0, 0)),
            pl.BlockSpec((1, d), lambda bi, i, j: (0, 0)),
            pl.BlockSpec((d, tn), lambda bi, i, j: (0, j)),
            pl.BlockSpec((tm, LANE), lambda bi, i, j: (i, 0)),
            pl.BlockSpec((tm, LANE), lambda bi, i, j: (i, 0)),
        ],
        out_specs=pl.BlockSpec((1, tm, tn), lambda bi, i, j: (bi, i, j)),
        out_shape=jax.ShapeDtypeStruct((b, t, n), BF16),
        scratch_shapes=[pltpu.VMEM((tm, d), BF16)],
        compiler_params=_params(("parallel", "parallel", "arbitrary"), 48),
        name="proj",
    )(xa, mod, g, w, cos, sin)


PEAK_LIMIT = 2.0 ** 64
ONES = 16


def _diff_attn_kernel(lamp_ref, q_ref, k_ref, vt_ref, z_ref, sg_ref, o_ref, acc_ref,
                      *, tq, tk, chunks, lam_init):
    q = q_ref[0].astype(F32)
    lane = lax.broadcasted_iota(jnp.int32, q.shape, 1)
    map0 = lane % HALF < HALF // 2
    qm = jnp.concatenate([jnp.where(map0, q, 0.0),
                          jnp.where(map0, 0.0, q)], axis=0).astype(BF16)
    lp = lamp_ref[...]
    lam = (jnp.exp(jnp.sum(lp[0:1] * lp[1:2], axis=1, keepdims=True))
           - jnp.exp(jnp.sum(lp[2:3] * lp[3:4], axis=1, keepdims=True)) + lam_init)

    def scores(c0, cl):
        return lax.dot_general(k_ref[0, pl.ds(c0, cl), :], qm, (((1,), (1,)), ((), ())),
                               preferred_element_type=F32)

    def pv(e, c0, cl):
        return jnp.dot(vt_ref[0, 0, :, pl.ds(c0, cl)], e.astype(BF16),
                       preferred_element_type=F32)

    def finish():
        acc = acc_ref[...]
        inv = 1.0 / acc[DA_VD:DA_VD + 1, :]
        o_t = (acc[:DA_VD, :tq] * inv[:, :tq]
               - lam * (acc[:DA_VD, tq:] * inv[:, tq:]))
        o = o_t.T
        o = o * lax.rsqrt(jnp.mean(o * o, axis=-1, keepdims=True) + EPS) * sg_ref[...]
        o = o * (1.0 - lam_init)
        o_ref[0] = (o * _silu(z_ref[0].astype(F32))).astype(o_ref.dtype)

    m = peak = None
    for c0, cl in chunks:
        s = scores(c0, cl)
        if m is None:
            m = jnp.max(s[0:8], axis=0, keepdims=True)
        e = jnp.exp2(s - m).astype(BF16)
        emax = jnp.max(e, axis=0, keepdims=True).astype(F32)
        up = jnp.maximum(emax, 1.0)
        alpha = 1.0 / up
        if peak is None:
            acc_ref[...] = pv(e, c0, cl) * alpha
            peak = emax
        else:
            acc_ref[...] = (acc_ref[...] + pv(e, c0, cl)) * alpha
            peak = jnp.maximum(peak, emax)
        m = m + jnp.log2(up)
    finish()

    @pl.when(jnp.logical_not(jnp.max(peak) <= PEAK_LIMIT))
    def _():
        s = scores(*chunks[0])
        m = jnp.max(s, axis=0, keepdims=True)
        acc_ref[...] = pv(jnp.exp2(s - m), *chunks[0])

        def body(c, m):
            c0 = pl.multiple_of(chunks[1][0] + c * tk, tk)
            s = scores(c0, tk)
            m_new = jnp.maximum(m, jnp.max(s, axis=0, keepdims=True))
            acc_ref[...] = jnp.exp2(m - m_new) * acc_ref[...] + pv(jnp.exp2(s - m_new), c0, tk)
            return m_new

        if len(chunks) > 1:
            lax.fori_loop(0, len(chunks) - 1, body, m)
        finish()


def _diff_attn(qkvz, vt, lamp, sg, *, seq, n_ctx, lam_init, ctx_queries, tq=512, tk=512):
    b, t, _ = qkvz.shape
    h = DA_HEADS
    if ctx_queries:
        tq, chunks = n_ctx, ((0, n_ctx),)
        q_blk0, k_rows, k_blk = seq // n_ctx, n_ctx, seq // n_ctx
        out_rows = n_ctx
    else:
        chunks = ((seq, n_ctx),) + tuple((c, tk) for c in range(0, seq, tk))
        q_blk0, k_rows, k_blk = 0, t, 0
        out_rows = seq
    kern = functools.partial(_diff_attn_kernel, tq=tq, tk=tk, chunks=chunks, lam_init=lam_init)
    return pl.pallas_call(
        kern,
        grid=(b, h, out_rows // tq),
        in_specs=[
            pl.BlockSpec((4, DA_HD), lambda bi, hi, i: (0, 0)),
            pl.BlockSpec((1, tq, LANE), lambda bi, hi, i: (bi, q_blk0 + i, hi)),
            pl.BlockSpec((1, k_rows, LANE), lambda bi, hi, i: (bi, k_blk, h + hi)),
            pl.BlockSpec((1, 1, DA_VD + ONES, k_rows), lambda bi, hi, i: (bi, hi, 0, k_blk)),
            pl.BlockSpec((1, tq, LANE), lambda bi, hi, i: (bi, q_blk0 + i, 3 * h + hi)),
            pl.BlockSpec((1, DA_VD), lambda bi, hi, i: (0, 0)),
        ],
        out_specs=pl.BlockSpec((1, tq, LANE), lambda bi, hi, i: (bi, i, hi)),
        out_shape=jax.ShapeDtypeStruct((b, out_rows, h * DA_VD), BF16),
        scratch_shapes=[pltpu.VMEM((DA_VD + ONES, 2 * tq), F32)],
        compiler_params=_params(("parallel", "parallel", "arbitrary"), 48),
        name="diff_attn_ctx" if ctx_queries else "diff_attn",
    )(lamp, qkvz, qkvz, vt, qkvz, sg)


def _swa_kernel(q_ref, k_ref, vt_ref, z_ref, sink_ref, o_ref, *, tq, seq, n_ctx, n_lat_tiles):
    i = pl.program_id(2)
    span = tq + 2 * WINDOW
    q4 = jnp.concatenate([q_ref[0, :, g * WC_HD:(g + 1) * WC_HD] for g in range(WC_G)],
                         axis=0)
    sink = sink_ref[0]
    nt = (((1,), (1,)), ((), ()))
    s_ctx = lax.dot_general(k_ref[0, seq:seq + n_ctx, :], q4, nt,
                            preferred_element_type=F32)

    def finish(o_t, den):
        o_t = o_t * (1.0 / den)
        for g in range(WC_G):
            o = o_t[:, g * tq:(g + 1) * tq].T
            z = z_ref[0, :, g * WC_HD:(g + 1) * WC_HD].astype(F32)
            o_ref[0, :, g * WC_HD:(g + 1) * WC_HD] = (o * _silu(z)).astype(o_ref.dtype)

    def attend(s_lat, base):
        e_ctx = jnp.exp2(s_ctx).astype(BF16)
        acc = jnp.dot(vt_ref[0, 0, :, seq:seq + n_ctx], e_ctx, preferred_element_type=F32)
        peak = jnp.max(e_ctx, axis=0, keepdims=True).astype(F32)
        if s_lat is not None:
            e_lat = jnp.exp2(s_lat).astype(BF16)
            acc = acc + jnp.dot(vt_ref[0, 0, :, pl.ds(base, span)], e_lat,
                                preferred_element_type=F32)
            peak = jnp.maximum(peak, jnp.max(e_lat, axis=0, keepdims=True).astype(F32))
        e_sink = jnp.exp2(sink)
        den = acc[WC_HD:WC_HD + 1, :] + e_sink
        finish(acc[:WC_HD, :], den)
        in_range = jnp.logical_and(jnp.max(jnp.maximum(peak, e_sink)) <= PEAK_LIMIT,
                                   jnp.min(den) >= 1.0 / PEAK_LIMIT)

        @pl.when(jnp.logical_not(in_range))
        def _():
            m = jnp.maximum(jnp.max(s_ctx, axis=0, keepdims=True), sink)
            if s_lat is not None:
                m = jnp.maximum(m, jnp.max(s_lat, axis=0, keepdims=True))
            e_c = jnp.exp2(s_ctx - m)
            den = jnp.sum(e_c, axis=0, keepdims=True) + jnp.exp2(sink - m)
            o_t = jnp.dot(vt_ref[0, 0, :WC_HD, seq:seq + n_ctx], e_c.astype(BF16),
                          preferred_element_type=F32)
            if s_lat is not None:
                e_l = jnp.exp2(s_lat - m)
                den = den + jnp.sum(e_l, axis=0, keepdims=True)
                o_t = o_t + jnp.dot(vt_ref[0, 0, :WC_HD, pl.ds(base, span)], e_l.astype(BF16),
                                    preferred_element_type=F32)
            finish(o_t, den)

    @pl.when(i < n_lat_tiles)
    def _():
        start = i * tq
        base = pl.multiple_of(jnp.clip(start - WINDOW, 0, seq - span), LANE)
        s_lat = lax.dot_general(k_ref[0, pl.ds(base, span), :], q4, nt,
                                preferred_element_type=F32)
        kpos = base + lax.broadcasted_iota(jnp.int32, (span, 1), 0)
        qpos = start + lax.broadcasted_iota(jnp.int32, (1, tq), 1)
        bias = jnp.where(jnp.abs(qpos - kpos) <= WINDOW, 0.0, -jnp.inf)
        attend(s_lat + jnp.concatenate([bias] * WC_G, axis=1), base)

    @pl.when(i >= n_lat_tiles)
    def _():
        attend(None, None)


def _swa_attn(qkvz, vt, sink_rows, *, seq, n_ctx, tq=256):
    b, t, _ = qkvz.shape
    gw = WC_G * WC_HD
    n_lat = seq // tq
    kern = functools.partial(_swa_kernel, tq=tq, seq=seq, n_ctx=n_ctx, n_lat_tiles=n_lat)
    k_blk0 = WC_KV * gw // WC_HD
    z_blk0 = (WC_KV * gw + 2 * WC_KV * WC_HD) // gw
    return pl.pallas_call(
        kern,
        grid=(b, WC_KV, t // tq),
        in_specs=[
            pl.BlockSpec((1, tq, gw), lambda bi, n, i: (bi, i, n)),
            pl.BlockSpec((1, t, WC_HD), lambda bi, n, i: (bi, 0, k_blk0 + n)),
            pl.BlockSpec((1, 1, WC_HD + ONES, t), lambda bi, n, i: (bi, n, 0, 0)),
            pl.BlockSpec((1, tq, gw), lambda bi, n, i: (bi, i, z_blk0 + n)),
            pl.BlockSpec((1, 1, WC_G * tq), lambda bi, n, i: (n, 0, 0)),
        ],
        out_specs=pl.BlockSpec((1, tq, gw), lambda bi, n, i: (bi, i, n)),
        out_shape=jax.ShapeDtypeStruct((b, t, WC_KV * gw), BF16),
        compiler_params=_params(("parallel", "parallel", "arbitrary"), 48),
        name="swa_attn",
    )(qkvz, qkvz, vt, qkvz, sink_rows)


HALO = 8


def _pool_kernel(x_ref, xp_ref, xn_ref, mod_ref, g_ref, w_ref, wg_ref, bg_ref, sc_ref,
                 o_ref, hb_ref, s2_ref, s4_ref, s8_ref, *, tm, seq, n_ctx, n_lat_tiles):
    i = pl.program_id(1)
    is_ctx = i >= n_lat_tiles
    shift = jnp.where(is_ctx, mod_ref[0, 3:4, :], mod_ref[0, 0:1, :])
    scale = jnp.where(is_ctx, mod_ref[0, 4:5, :], mod_ref[0, 1:2, :])
    g = g_ref[...]
    has_prev = jnp.logical_and(i != 0, i != n_lat_tiles)
    has_next = jnp.logical_and(i != n_lat_tiles - 1, i < n_lat_tiles)
    h = _norm_mod(x_ref[0], g, shift, scale)
    hb_ref[HALO:HALO + tm, :] = h
    hb_ref[0:HALO, :] = jnp.where(has_prev, _norm_mod(xp_ref[0], g, shift, scale), 0.0)
    hb_ref[HALO + tm:, :] = jnp.where(has_next, _norm_mod(xn_ref[0], g, shift, scale), 0.0)
    n = tm + 2 * HALO
    s2_ref[1:n, :] = hb_ref[0:n - 1, :] + hb_ref[1:n, :]
    s4_ref[2:n - 1, :] = s2_ref[1:n - 2, :] + s2_ref[3:n, :]
    s8_ref[4:n - 3, :] = s4_ref[2:n - 5, :] + s4_ref[6:n - 1, :]
    sums = (s2_ref[HALO:HALO + tm, :], s4_ref[HALO:HALO + tm, :], s8_ref[HALO:HALO + tm, :],
            s8_ref[HALO - 4:HALO - 4 + tm, :] + s8_ref[HALO + 4:HALO + 4 + tm, :])
    pos = jnp.where(is_ctx, 0, i * tm) + lax.broadcasted_iota(jnp.int32, (tm, 1), 0)
    seg_len = jnp.where(is_ctx, n_ctx, seq)
    hb16 = h.astype(BF16)
    d_inner = len(POOL_WINDOWS) * POOL_GW
    for gi, w in enumerate(POOL_WINDOWS):
        lo = jnp.maximum(pos - w // 2, 0)
        hi = jnp.minimum(pos - w // 2 + w, seg_len)
        cnt = (hi - lo).astype(F32)
        hd = (sums[gi] / cnt - h).astype(BF16)
        cols = slice(gi * POOL_GW, (gi + 1) * POOL_GW)
        d = jnp.dot(hd, w_ref[:, cols], preferred_element_type=F32)
        y = jnp.dot(d.astype(BF16), wg_ref[gi], preferred_element_type=F32) + bg_ref[:, cols]
        y = y * sc_ref[:, cols]
        z = jnp.dot(hb16, w_ref[:, d_inner + gi * POOL_GW:d_inner + (gi + 1) * POOL_GW],
                    preferred_element_type=F32)
        o_ref[0, :, cols] = (y * _silu(z)).astype(o_ref.dtype)


def _pool_mixer(xa, mod, g, w_in, w_grp, b_grp, scale, *, seq, n_ctx, tm=256):
    b, t, d = xa.shape
    d_inner = len(POOL_WINDOWS) * POOL_GW
    hb = tm // HALO
    last = t // HALO - 1
    kern = functools.partial(_pool_kernel, tm=tm, seq=seq, n_ctx=n_ctx, n_lat_tiles=seq // tm)
    buf = pltpu.VMEM((tm + 2 * HALO, d), F32)
    return pl.pallas_call(
        kern,
        grid=(b, t // tm),
        in_specs=[
            pl.BlockSpec((1, tm, d), lambda bi, i: (bi, i, 0)),
            pl.BlockSpec((1, HALO, d), lambda bi, i: (bi, jnp.maximum(i * hb - 1, 0), 0)),
            pl.BlockSpec((1, HALO, d), lambda bi, i: (bi, jnp.minimum((i + 1) * hb, last), 0)),
            pl.BlockSpec((1, 8, d), lambda bi, i: (bi, 0, 0)),
            pl.BlockSpec((1, d), lambda bi, i: (0, 0)),
            pl.BlockSpec((d, 2 * d_inner), lambda bi, i: (0, 0)),
            pl.BlockSpec((len(POOL_WINDOWS), POOL_GW, POOL_GW), lambda bi, i: (0, 0, 0)),
            pl.BlockSpec((1, d_inner), lambda bi, i: (0, 0)),
            pl.BlockSpec((1, d_inner), lambda bi, i: (0, 0)),
        ],
        out_specs=pl.BlockSpec((1, tm, d_inner), lambda bi, i: (bi, i, 0)),
        out_shape=jax.ShapeDtypeStruct((b, t, d_inner), BF16),
        scratch_shapes=[buf, buf, buf, buf],
        compiler_params=_params(("parallel", "arbitrary"), 56),
        name="pool_mixer",
    )(xa, xa, xa, mod, g, w_in, w_grp, b_grp, scale)


def _out_kernel(g_ref, gc_ref, w_ref, x_ref, mod_ref, fg_ref, o_ref, *, tm, seq, final, split):
    i = pl.program_id(1)
    is_ctx = i * tm >= seq

    def emit(g, gate):
        y = jnp.dot(g, w_ref[...], preferred_element_type=F32)
        xn = x_ref[0] + gate * y
        if final:
            xn = xn * lax.rsqrt(jnp.mean(xn * xn, axis=-1, keepdims=True) + EPS) * fg_ref[...]
        o_ref[0] = xn

    if split:
        @pl.when(jnp.logical_not(is_ctx))
        def _():
            emit(g_ref[0], mod_ref[0, 2:3, :])

        @pl.when(is_ctx)
        def _():
            emit(gc_ref[0], mod_ref[0, 5:6, :])
    else:
        row = i * tm + lax.broadcasted_iota(jnp.int32, (tm, 1), 0)
        emit(g_ref[0], jnp.where(row >= seq, mod_ref[0, 5:6, :], mod_ref[0, 2:3, :]))


def _out_proj(gact, gctx, w, xa, mod, final_g, *, seq, final, tm):
    b, _, d_inner = gact.shape
    d = w.shape[1]
    rows = seq if final else xa.shape[1]
    split = gctx is not None
    if split:
        assert seq % tm == 0 and gctx.shape[1] == tm
        last_lat = seq // tm - 1
        g_map = lambda bi, i: (bi, jnp.minimum(i, last_lat), 0)
    else:
        gctx = gact
        g_map = lambda bi, i: (bi, i, 0)
    kern = functools.partial(_out_kernel, tm=tm, seq=seq, final=final, split=split)
    return pl.pallas_call(
        kern,
        grid=(b, rows // tm),
        in_specs=[
            pl.BlockSpec((1, tm, d_inner), g_map),
            pl.BlockSpec((1, tm, d_inner), lambda bi, i: (bi, 0, 0)),
            pl.BlockSpec((d_inner, d), lambda bi, i: (0, 0)),
            pl.BlockSpec((1, tm, d), lambda bi, i: (bi, i, 0)),
            pl.BlockSpec((1, 8, d), lambda bi, i: (bi, 0, 0)),
            pl.BlockSpec((1, d), lambda bi, i: (0, 0)),
        ],
        out_specs=pl.BlockSpec((1, tm, d), lambda bi, i: (bi, i, 0)),
        out_shape=jax.ShapeDtypeStruct((b, rows, d), F32),
        compiler_params=_params(("parallel", "parallel"), 48),
        name="out_proj",
    )(gact, gctx, w, xa, mod, final_g)


def kernel(x, c, ctx, c_ctx, norm_g, w_ada, b_ada, a_w_in, a_w_out, a_lam_q1, a_lam_k1, a_lam_q2, a_lam_k2, a_subln_g, b_w_in, b_w_grp, b_b_grp, b_scale, b_w_out, c_w_in, c_sink, c_w_out, final_g):
    b, seq, d = x.shape
    n_ctx = ctx.shape[1]
    depth = w_ada.shape[0]
    xa = jnp.concatenate([x, ctx], axis=1)

    cond8 = jnp.concatenate([c, c_ctx[None, :], jnp.zeros((8 - b - 1, d), F32)], axis=0)
    ada = _adaln(cond8, w_ada, b_ada)

    cos_a, sin_a = _rope_tables(seq, n_ctx, DA_HD)
    cos_c, sin_c = _rope_tables(seq, n_ctx, WC_HD)

    out = None
    for i in range(depth):
        m = i % N_MIXERS
        j = i // N_MIXERS
        last = i == depth - 1
        lat = ada[i, :b].reshape(b, 3, d)
        cm = jnp.broadcast_to(ada[i, b].reshape(1, 3, d), (b, 3, d))
        mod = jnp.concatenate([lat, cm, jnp.zeros((b, 2, d), F32)], axis=1)
        g = norm_g[i].reshape(1, d)
        gctx = None

        if m == 0:
            lam_init = 0.8 - 0.6 * math.exp(-0.3 * i)
            n_qk = DA_HEADS * 2 * DA_HD
            w_in = _permute_rope_columns(a_w_in[j], 2 * n_qk, DA_HD).astype(BF16)
            qkvz = _proj(xa, mod, g, w_in, cos_a, sin_a, seq=seq, n_q=n_qk, n_k=n_qk,
                         qscale=DA_HD ** -0.5 * LOG2E)
            v0 = 2 * DA_HEADS * DA_VD
            vt = qkvz[:, :, v0:v0 + DA_HEADS * DA_VD].reshape(b, -1, DA_HEADS, DA_VD)
            vt = vt.transpose(0, 2, 3, 1)
            vt = jnp.concatenate([vt, jnp.ones((b, DA_HEADS, ONES, vt.shape[-1]), BF16)], axis=2)
            lamp = jnp.stack([a_lam_q1[j], a_lam_k1[j], a_lam_q2[j], a_lam_k2[j]], axis=0)
            sg = a_subln_g[j].reshape(1, DA_VD)
            gact = _diff_attn(qkvz, vt, lamp, sg, seq=seq, n_ctx=n_ctx, lam_init=lam_init,
                              ctx_queries=False)
            if not last:
                gctx = _diff_attn(qkvz, vt, lamp, sg, seq=seq, n_ctx=n_ctx, lam_init=lam_init,
                                  ctx_queries=True)
            w_out = a_w_out[j]
        elif m == 1:
            gact = _pool_mixer(xa, mod, g, b_w_in[j].astype(BF16), b_w_grp[j].astype(BF16),
                               b_b_grp[j].reshape(1, -1), b_scale[j].reshape(1, -1),
                               seq=seq, n_ctx=n_ctx)
            w_out = b_w_out[j]
        else:
            n_q, n_k = WC_KV * WC_G * WC_HD, WC_KV * WC_HD
            w_in = _permute_rope_columns(c_w_in[j], n_q + n_k, WC_HD).astype(BF16)
            qkvz = _proj(xa, mod, g, w_in, cos_c, sin_c, seq=seq, n_q=n_q, n_k=n_k,
                         qscale=WC_HD ** -0.5 * LOG2E)
            v0 = WC_KV * WC_G * WC_HD + WC_KV * WC_HD
            vt = qkvz[:, :, v0:v0 + WC_KV * WC_HD].reshape(b, -1, WC_KV, WC_HD)
            vt = vt.transpose(0, 2, 3, 1)
            vt = jnp.concatenate([vt, jnp.ones((b, WC_KV, ONES, vt.shape[-1]), BF16)], axis=2)
            tq = 256
            sink_rows = jnp.repeat(c_sink[j].astype(F32) * LOG2E, tq).reshape(WC_KV, 1, WC_G * tq)
            gact = _swa_attn(qkvz, vt, sink_rows, seq=seq, n_ctx=n_ctx, tq=tq)
            w_out = c_w_out[j]

        if last:
            tm = 512
        else:
            tm = 544 if gctx is None else n_ctx
        res = _out_proj(gact, gctx, w_out.astype(BF16), xa, mod, final_g.reshape(1, d), seq=seq,
                        final=last, tm=tm)
        if last:
            out = res
        else:
            xa = res
    return out
```

```python
import functools
import math

import jax
import jax.numpy as jnp
from jax import lax
from jax.experimental import pallas as pl
from jax.experimental.pallas import tpu as pltpu

F32 = jnp.float32
BF16 = jnp.bfloat16

GRID_W = 64
EPS = 1e-6
ROPE_BASE = 10000.0
N_MIXERS = 3
DA_HEADS = 16
DA_HD = 64
DA_VD = 128
POOL_WINDOWS = (2, 4, 8, 16)
POOL_GW = 512
WC_HD = 128
WC_KV = 4
WC_G = 4
WINDOW = 128

LOG2E = 1.4426950408889634
LANE = 128
MIB = 1024 * 1024


def _params(semantics, vmem_mib):
    return pltpu.CompilerParams(dimension_semantics=semantics,
                                vmem_limit_bytes=vmem_mib * MIB)


def _silu(v):
    return v * jax.nn.sigmoid(v)


def _adaln_kernel(c_ref, w_ref, b_ref, o_ref):
    a = _silu(c_ref[...]).astype(BF16)
    o_ref[0] = jnp.dot(a, w_ref[0].astype(BF16), preferred_element_type=F32) + b_ref[0]


def _adaln(cond8, w_ada, b_ada):
    depth, d, d3 = w_ada.shape
    nt = d3 // d
    return pl.pallas_call(
        _adaln_kernel,
        grid=(depth, nt),
        in_specs=[
            pl.BlockSpec((8, d), lambda l, n: (0, 0)),
            pl.BlockSpec((1, d, d), lambda l, n: (l, 0, n)),
            pl.BlockSpec((1, 1, d), lambda l, n: (l, 0, n)),
        ],
        out_specs=pl.BlockSpec((1, 8, d), lambda l, n: (l, 0, n)),
        out_shape=jax.ShapeDtypeStruct((depth, 8, d3), F32),
        compiler_params=_params(("parallel", "parallel"), 32),
        name="adaln",
    )(cond8, w_ada, b_ada.reshape(depth, 1, d3))


def _norm_mod(xf, g, shift, scale):
    y = xf * lax.rsqrt(jnp.mean(xf * xf, axis=-1, keepdims=True) + EPS) * g
    return y * (1.0 + scale) + shift


SUB = 256
HALF = LANE // 2


def _rope_lane_layout(head_dim):
    quarter = head_dim // 4
    lane = jnp.arange(LANE)
    second = lane // HALF
    grp = (lane % HALF) // quarter
    freq = lane % quarter
    axis = grp % 2
    unit = grp // 2
    old = unit * head_dim + (axis * 2 + second) * quarter + freq
    sign = jnp.where(second == 0, -1.0, 1.0).astype(F32)
    return old, axis, freq, sign


def _rope_tables(seq, n_ctx, head_dim):
    quarter = head_dim // 4
    _, axis, freq, sign = _rope_lane_layout(head_dim)
    inv = ROPE_BASE ** (-jnp.arange(quarter, dtype=F32) / quarter)
    pos = jnp.arange(seq)
    coord = jnp.where(axis[None, :] == 0, (pos // GRID_W)[:, None], (pos % GRID_W)[:, None])
    ang = coord.astype(F32) * inv[freq][None, :]
    cos = jnp.concatenate([jnp.cos(ang), jnp.ones((n_ctx, LANE), F32)], axis=0)
    sin = jnp.concatenate([jnp.sin(ang) * sign[None, :], jnp.zeros((n_ctx, LANE), F32)], axis=0)
    return cos, sin


def _permute_rope_columns(w, n_cols, head_dim):
    quarter = head_dim // 4
    units = LANE // head_dim
    d = w.shape[0]
    head = w[:, :n_cols].reshape(d, n_cols // LANE, units, 2, 2, quarter)
    head = head.transpose(0, 1, 4, 2, 3, 5).reshape(d, n_cols)
    return jnp.concatenate([head, w[:, n_cols:]], axis=1)


def _proj_kernel(x_ref, mod_ref, g_ref, w_ref, cos_ref, sin_ref, o_ref, h_ref,
                 *, tm, seq, groups, qscale):
    i = pl.program_id(1)
    j = pl.program_id(2)

    @pl.when(j == 0)
    def _():
        row = i * tm + lax.broadcasted_iota(jnp.int32, (tm, 1), 0)
        is_ctx = row >= seq
        shift = jnp.where(is_ctx, mod_ref[0, 3:4, :], mod_ref[0, 0:1, :])
        scale = jnp.where(is_ctx, mod_ref[0, 4:5, :], mod_ref[0, 1:2, :])
        h_ref[...] = _norm_mod(x_ref[0], g_ref[...], shift, scale).astype(BF16)

    def emit(kinds):
        tables = {}
        for kind in set(kinds) - {"P"}:
            mul = qscale if kind == "Q" else 1.0
            tables[kind] = (cos_ref[...] * mul, sin_ref[...] * mul)
        for s, kind in enumerate(kinds):
            acc = jnp.dot(h_ref[...], w_ref[:, s * SUB:(s + 1) * SUB],
                          preferred_element_type=F32)
            if kind == "P":
                o_ref[0, :, s * SUB:(s + 1) * SUB] = acc.astype(o_ref.dtype)
                continue
            cos, sin = tables[kind]
            for u in range(SUB // LANE):
                t = acc[:, u * LANE:(u + 1) * LANE]
                c0 = s * SUB + u * LANE
                o_ref[0, :, c0:c0 + LANE] = (t * cos + pltpu.roll(t, HALF, 1) * sin
                                             ).astype(o_ref.dtype)

    for j_lo, j_hi, kinds in groups:
        pl.when(jnp.logical_and(j >= j_lo, j <= j_hi))(functools.partial(emit, kinds))


def _proj(xa, mod, g, w, cos, sin, *, seq, n_q, n_k, qscale, tm=1088, tn=1024):
    b, t, d = xa.shape
    n = w.shape[1]
    per_tile = tn // SUB
    kinds = ["Q"] * (n_q // SUB) + ["K"] * (n_k // SUB) + ["P"] * ((n - n_q - n_k) // SUB)
    tiles = [tuple(kinds[jt * per_tile:(jt + 1) * per_tile]) for jt in range(n // tn)]
    groups = []
    for jt, kt in enumerate(tiles):
        if groups and groups[-1][2] == kt:
            groups[-1] = (groups[-1][0], jt, kt)
        else:
            groups.append((jt, jt, kt))
    kern = functools.partial(_proj_kernel, tm=tm, seq=seq, groups=tuple(groups), qscale=qscale)
    return pl.pallas_call(
        kern,
        grid=(b, t // tm, n // tn),
        in_specs=[
            pl.BlockSpec((1, tm, d), lambda bi, i, j: (bi, i, 0)),
            pl.BlockSpec((1, 8, d), lambda bi, i, j: (bi, 0, 0)),
            pl.BlockSpec((1, d), lambda bi, i, j: (0, 0)),
            pl.BlockSpec((d, tn), lambda bi, i, j: (0, j)),
            pl.BlockSpec((tm, LANE), lambda bi, i, j: (i, 0)),
            pl.BlockSpec((tm, LANE), lambda bi, i, j: (i, 0)),
        ],
        out_specs=pl.BlockSpec((1, tm, tn), lambda bi, i, j: (bi, i, j)),
        out_shape=jax.ShapeDtypeStruct((b, t, n), BF16),
        scratch_shapes=[pltpu.VMEM((tm, d), BF16)],
        compiler_params=_params(("parallel", "parallel", "arbitrary"), 48),
        name="proj",
    )(xa, mod, g, w, cos, sin)


PEAK_LIMIT = 2.0 ** 64
ONES = 16


def _stage_vt(v_ref, vt_ref):
    rows, vd = v_ref.shape[1], v_ref.shape[2]
    vt_ref[0:vd, :] = v_ref[0].T
    vt_ref[vd:vd + ONES, :] = jnp.ones((ONES, rows), vt_ref.dtype)


def _diff_attn_kernel(lamp_ref, q_ref, k_ref, v_ref, z_ref, sg_ref, o_ref, acc_ref, vt_ref,
                      *, tq, nsub, tk, chunks, lam_init):
    @pl.when(pl.program_id(2) == 0)
    def _():
        _stage_vt(v_ref, vt_ref)

    lp = lamp_ref[...]
    lam = (jnp.exp(jnp.sum(lp[0:1] * lp[1:2], axis=1, keepdims=True))
           - jnp.exp(jnp.sum(lp[2:3] * lp[3:4], axis=1, keepdims=True)) + lam_init)

    def queries(u):
        q = q_ref[0, u * tq:(u + 1) * tq, :].astype(F32)
        lane = lax.broadcasted_iota(jnp.int32, q.shape, 1)
        map0 = lane % HALF < HALF // 2
        return jnp.concatenate([jnp.where(map0, q, 0.0),
                                jnp.where(map0, 0.0, q)], axis=0).astype(BF16)

    def scores(qm, c0, cl):
        return lax.dot_general(k_ref[0, pl.ds(c0, cl), :], qm, (((1,), (1,)), ((), ())),
                               preferred_element_type=F32)

    def pv(e, c0, cl):
        return jnp.dot(vt_ref[:, pl.ds(c0, cl)], e.astype(BF16),
                       preferred_element_type=F32)

    def finish(u):
        acc = acc_ref[u]
        inv = 1.0 / acc[DA_VD:DA_VD + 1, :]
        o_t = (acc[:DA_VD, :tq] * inv[:, :tq]
               - lam * (acc[:DA_VD, tq:] * inv[:, tq:]))
        o = o_t.T
        o = o * lax.rsqrt(jnp.mean(o * o, axis=-1, keepdims=True) + EPS) * sg_ref[...]
        o = o * (1.0 - lam_init)
        z = z_ref[0, u * tq:(u + 1) * tq, :].astype(F32)
        o_ref[0, u * tq:(u + 1) * tq, :] = (o * _silu(z)).astype(o_ref.dtype)

    def fast(u, qm):
        m = peak = None
        for c0, cl in chunks:
            s = scores(qm, c0, cl)
            if m is None:
                m = jnp.max(s[0:8], axis=0, keepdims=True)
            e = jnp.exp2(s - m).astype(BF16)
            emax = jnp.max(e, axis=0, keepdims=True).astype(F32)
            up = jnp.maximum(emax, 1.0)
            alpha = 1.0 / up
            if peak is None:
                acc_ref[u] = pv(e, c0, cl) * alpha
                peak = emax
            else:
                acc_ref[u] = (acc_ref[u] + pv(e, c0, cl)) * alpha
                peak = jnp.maximum(peak, emax)
            m = m + jnp.log2(up)
        finish(u)
        return jnp.max(peak)

    def exact(u, qm):
        s = scores(qm, *chunks[0])
        m = jnp.max(s, axis=0, keepdims=True)
        acc_ref[u] = pv(jnp.exp2(s - m), *chunks[0])

        def body(c, m):
            c0 = pl.multiple_of(chunks[1][0] + c * tk, tk)
            s = scores(qm, c0, tk)
            m_new = jnp.maximum(m, jnp.max(s, axis=0, keepdims=True))
            acc_ref[u] = jnp.exp2(m - m_new) * acc_ref[u] + pv(jnp.exp2(s - m_new), c0, tk)
            return m_new

        if len(chunks) > 1:
            lax.fori_loop(0, len(chunks) - 1, body, m)
        finish(u)

    qms = [queries(u) for u in range(nsub)]
    peak = functools.reduce(jnp.maximum, [fast(u, qms[u]) for u in range(nsub)])

    @pl.when(jnp.logical_not(peak <= PEAK_LIMIT))
    def _():
        for u in range(nsub):
            exact(u, qms[u])


def _diff_attn(qkvz, lamp, sg, *, seq, n_ctx, lam_init, ctx_queries, tq=512, nsub=2, tk=512):
    b, t, _ = qkvz.shape
    h = DA_HEADS
    if ctx_queries:
        tq, nsub, chunks = n_ctx, 1, ((0, n_ctx),)
        q_blk0, k_rows, k_blk = seq // n_ctx, n_ctx, seq // n_ctx
        out_rows = n_ctx
    else:
        chunks = ((seq, n_ctx),) + tuple((c, tk) for c in range(0, seq, tk))
        q_blk0, k_rows, k_blk = 0, t, 0
        out_rows = seq
    kern = functools.partial(_diff_attn_kernel, tq=tq, nsub=nsub, tk=tk, chunks=chunks,
                             lam_init=lam_init)
    tqs = tq * nsub
    return pl.pallas_call(
        kern,
        grid=(b, h, out_rows // tqs),
        in_specs=[
            pl.BlockSpec((4, DA_HD), lambda bi, hi, i: (0, 0)),
            pl.BlockSpec((1, tqs, LANE), lambda bi, hi, i: (bi, q_blk0 + i, hi)),
            pl.BlockSpec((1, k_rows, LANE), lambda bi, hi, i: (bi, k_blk, h + hi)),
            pl.BlockSpec((1, k_rows, LANE), lambda bi, hi, i: (bi, k_blk, 2 * h + hi)),
            pl.BlockSpec((1, tqs, LANE), lambda bi, hi, i: (bi, q_blk0 + i, 3 * h + hi)),
            pl.BlockSpec((1, DA_VD), lambda bi, hi, i: (0, 0)),
        ],
        out_specs=pl.BlockSpec((1, tqs, LANE), lambda bi, hi, i: (bi, i, hi)),
        out_shape=jax.ShapeDtypeStruct((b, out_rows, h * DA_VD), BF16),
        scratch_shapes=[pltpu.VMEM((nsub, DA_VD + ONES, 2 * tq), F32),
                        pltpu.VMEM((DA_VD + ONES, k_rows), BF16)],
        compiler_params=_params(("parallel", "parallel", "arbitrary"), 48),
        name="diff_attn_ctx" if ctx_queries else "diff_attn",
    )(lamp, qkvz, qkvz, qkvz, qkvz, sg)


def _swa_kernel(q_ref, k_ref, v_ref, z_ref, sink_ref, o_ref, vt_ref,
                *, tq, seq, n_ctx, n_lat_tiles):
    i = pl.program_id(2)

    @pl.when(i == 0)
    def _():
        _stage_vt(v_ref, vt_ref)

    span = tq + 2 * WINDOW
    q4 = jnp.concatenate([q_ref[0, :, g * WC_HD:(g + 1) * WC_HD] for g in range(WC_G)],
                         axis=0)
    sink = sink_ref[0]
    nt = (((1,), (1,)), ((), ()))
    s_ctx = lax.dot_general(k_ref[0, seq:seq + n_ctx, :], q4, nt,
                            preferred_element_type=F32)

    def finish(o_t, den):
        o_t = o_t * (1.0 / den)
        for g in range(WC_G):
            o = o_t[:, g * tq:(g + 1) * tq].T
            z = z_ref[0, :, g * WC_HD:(g + 1) * WC_HD].astype(F32)
            o_ref[0, :, g * WC_HD:(g + 1) * WC_HD] = (o * _silu(z)).astype(o_ref.dtype)

    def attend(s_lat, base):
        e_ctx = jnp.exp2(s_ctx).astype(BF16)
        acc = jnp.dot(vt_ref[:, seq:seq + n_ctx], e_ctx, preferred_element_type=F32)
        peak = jnp.max(e_ctx, axis=0, keepdims=True).astype(F32)
        if s_lat is not None:
            e_lat = jnp.exp2(s_lat).astype(BF16)
            acc = acc + jnp.dot(vt_ref[:, pl.ds(base, span)], e_lat,
                                preferred_element_type=F32)
            peak = jnp.maximum(peak, jnp.max(e_lat, axis=0, keepdims=True).astype(F32))
        e_sink = jnp.exp2(sink)
        den = acc[WC_HD:WC_HD + 1, :] + e_sink
        finish(acc[:WC_HD, :], den)
        in_range = jnp.logical_and(jnp.max(jnp.maximum(peak, e_sink)) <= PEAK_LIMIT,
                                   jnp.min(den) >= 1.0 / PEAK_LIMIT)

        @pl.when(jnp.logical_not(in_range))
        def _():
            m = jnp.maximum(jnp.max(s_ctx, axis=0, keepdims=True), sink)
            if s_lat is not None:
                m = jnp.maximum(m, jnp.max(s_lat, axis=0, keepdims=True))
            e_c = jnp.exp2(s_ctx - m)
            den = jnp.sum(e_c, axis=0, keepdims=True) + jnp.exp2(sink - m)
            o_t = jnp.dot(vt_ref[:WC_HD, seq:seq + n_ctx], e_c.astype(BF16),
                          preferred_element_type=F32)
            if s_lat is not None:
                e_l = jnp.exp2(s_lat - m)
                den = den + jnp.sum(e_l, axis=0, keepdims=True)
                o_t = o_t + jnp.dot(vt_ref[:WC_HD, pl.ds(base, span)], e_l.astype(BF16),
                                    preferred_element_type=F32)
            finish(o_t, den)

    @pl.when(i < n_lat_tiles)
    def _():
        start = i * tq
        base = pl.multiple_of(jnp.clip(start - WINDOW, 0, seq - span), LANE)
        s_lat = lax.dot_general(k_ref[0, pl.ds(base, span), :], q4, nt,
                                preferred_element_type=F32)
        kpos = base + lax.broadcasted_iota(jnp.int32, (span, 1), 0)
        qpos = start + lax.broadcasted_iota(jnp.int32, (1, tq), 1)
        bias = jnp.where(jnp.abs(qpos - kpos) <= WINDOW, 0.0, -jnp.inf)
        attend(s_lat + jnp.concatenate([bias] * WC_G, axis=1), base)

    @pl.when(i >= n_lat_tiles)
    def _():
        attend(None, None)


def _swa_attn(qkvz, sink_rows, *, seq, n_ctx, tq=256):
    b, t, _ = qkvz.shape
    gw = WC_G * WC_HD
    n_lat = seq // tq
    kern = functools.partial(_swa_kernel, tq=tq, seq=seq, n_ctx=n_ctx, n_lat_tiles=n_lat)
    k_blk0 = WC_KV * gw // WC_HD
    z_blk0 = (WC_KV * gw + 2 * WC_KV * WC_HD) // gw
    return pl.pallas_call(
        kern,
        grid=(b, WC_KV, t // tq),
        in_specs=[
            pl.BlockSpec((1, tq, gw), lambda bi, n, i: (bi, i, n)),
            pl.BlockSpec((1, t, WC_HD), lambda bi, n, i: (bi, 0, k_blk0 + n)),
            pl.BlockSpec((1, t, WC_HD), lambda bi, n, i: (bi, 0, k_blk0 + WC_KV + n)),
            pl.BlockSpec((1, tq, gw), lambda bi, n, i: (bi, i, z_blk0 + n)),
            pl.BlockSpec((1, 1, WC_G * tq), lambda bi, n, i: (n, 0, 0)),
        ],
        out_specs=pl.BlockSpec((1, tq, gw), lambda bi, n, i: (bi, i, n)),
        out_shape=jax.ShapeDtypeStruct((b, t, WC_KV * gw), BF16),
        scratch_shapes=[pltpu.VMEM((WC_HD + ONES, t), BF16)],
        compiler_params=_params(("parallel", "parallel", "arbitrary"), 48),
        name="swa_attn",
    )(qkvz, qkvz, qkvz, qkvz, sink_rows)


HALO = 8


def _pool_kernel(x_ref, xp_ref, xn_ref, mod_ref, g_ref, w_ref, wg_ref, bg_ref, sc_ref,
                 o_ref, hb_ref, s2_ref, s4_ref, s8_ref, *, tm, seq, n_ctx, n_lat_tiles):
    i = pl.program_id(1)
    is_ctx = i >= n_lat_tiles
    shift = jnp.where(is_ctx, mod_ref[0, 3:4, :], mod_ref[0, 0:1, :])
    scale = jnp.where(is_ctx, mod_ref[0, 4:5, :], mod_ref[0, 1:2, :])
    g = g_ref[...]
    has_prev = jnp.logical_and(i != 0, i != n_lat_tiles)
    has_next = jnp.logical_and(i != n_lat_tiles - 1, i < n_lat_tiles)
    h = _norm_mod(x_ref[0], g, shift, scale)
    hb_ref[HALO:HALO + tm, :] = h
    hb_ref[0:HALO, :] = jnp.where(has_prev, _norm_mod(xp_ref[0], g, shift, scale), 0.0)
    hb_ref[HALO + tm:, :] = jnp.where(has_next, _norm_mod(xn_ref[0], g, shift, scale), 0.0)
    n = tm + 2 * HALO
    s2_ref[1:n, :] = hb_ref[0:n - 1, :] + hb_ref[1:n, :]
    s4_ref[2:n - 1, :] = s2_ref[1:n - 2, :] + s2_ref[3:n, :]
    s8_ref[4:n - 3, :] = s4_ref[2:n - 5, :] + s4_ref[6:n - 1, :]
    sums = (s2_ref[HALO:HALO + tm, :], s4_ref[HALO:HALO + tm, :], s8_ref[HALO:HALO + tm, :],
            s8_ref[HALO - 4:HALO - 4 + tm, :] + s8_ref[HALO + 4:HALO + 4 + tm, :])
    pos = jnp.where(is_ctx, 0, i * tm) + lax.broadcasted_iota(jnp.int32, (tm, 1), 0)
    seg_len = jnp.where(is_ctx, n_ctx, seq)
    hb16 = h.astype(BF16)
    d_inner = len(POOL_WINDOWS) * POOL_GW
    for gi, w in enumerate(POOL_WINDOWS):
        lo = jnp.maximum(pos - w // 2, 0)
        hi = jnp.minimum(pos - w // 2 + w, seg_len)
        cnt = (hi - lo).astype(F32)
        hd = (sums[gi] / cnt - h).astype(BF16)
        cols = slice(gi * POOL_GW, (gi + 1) * POOL_GW)
        d = jnp.dot(hd, w_ref[:, cols], preferred_element_type=F32)
        y = jnp.dot(d.astype(BF16), wg_ref[gi], preferred_element_type=F32) + bg_ref[:, cols]
        y = y * sc_ref[:, cols]
        z = jnp.dot(hb16, w_ref[:, d_inner + gi * POOL_GW:d_inner + (gi + 1) * POOL_GW],
                    preferred_element_type=F32)
        o_ref[0, :, cols] = (y * _silu(z)).astype(o_ref.dtype)


def _pool_mixer(xa, mod, g, w_in, w_grp, b_grp, scale, *, seq, n_ctx, tm=256):
    b, t, d = xa.shape
    d_inner = len(POOL_WINDOWS) * POOL_GW
    hb = tm // HALO
    last = t // HALO - 1
    kern = functools.partial(_pool_kernel, tm=tm, seq=seq, n_ctx=n_ctx, n_lat_tiles=seq // tm)
    buf = pltpu.VMEM((tm + 2 * HALO, d), F32)
    return pl.pallas_call(
        kern,
        grid=(b, t // tm),
        in_specs=[
            pl.BlockSpec((1, tm, d), lambda bi, i: (bi, i, 0)),
            pl.BlockSpec((1, HALO, d), lambda bi, i: (bi, jnp.maximum(i * hb - 1, 0), 0)),
            pl.BlockSpec((1, HALO, d), lambda bi, i: (bi, jnp.minimum((i + 1) * hb, last), 0)),
            pl.BlockSpec((1, 8, d), lambda bi, i: (bi, 0, 0)),
            pl.BlockSpec((1, d), lambda bi, i: (0, 0)),
            pl.BlockSpec((d, 2 * d_inner), lambda bi, i: (0, 0)),
            pl.BlockSpec((len(POOL_WINDOWS), POOL_GW, POOL_GW), lambda bi, i: (0, 0, 0)),
            pl.BlockSpec((1, d_inner), lambda bi, i: (0, 0)),
            pl.BlockSpec((1, d_inner), lambda bi, i: (0, 0)),
        ],
        out_specs=pl.BlockSpec((1, tm, d_inner), lambda bi, i: (bi, i, 0)),
        out_shape=jax.ShapeDtypeStruct((b, t, d_inner), BF16),
        scratch_shapes=[buf, buf, buf, buf],
        compiler_params=_params(("parallel", "arbitrary"), 56),
        name="pool_mixer",
    )(xa, xa, xa, mod, g, w_in, w_grp, b_grp, scale)


def _out_kernel(g_ref, gc_ref, w_ref, x_ref, mod_ref, fg_ref, o_ref, *, tm, seq, final, split):
    i = pl.program_id(1)
    is_ctx = i * tm >= seq

    def emit(g, gate):
        y = jnp.dot(g, w_ref[...], preferred_element_type=F32)
        xn = x_ref[0] + gate * y
        if final:
            xn = xn * lax.rsqrt(jnp.mean(xn * xn, axis=-1, keepdims=True) + EPS) * fg_ref[...]
        o_ref[0] = xn

    if split:
        @pl.when(jnp.logical_not(is_ctx))
        def _():
            emit(g_ref[0], mod_ref[0, 2:3, :])

        @pl.when(is_ctx)
        def _():
            emit(gc_ref[0], mod_ref[0, 5:6, :])
    else:
        row = i * tm + lax.broadcasted_iota(jnp.int32, (tm, 1), 0)
        emit(g_ref[0], jnp.where(row >= seq, mod_ref[0, 5:6, :], mod_ref[0, 2:3, :]))


def _out_proj(gact, gctx, w, xa, mod, final_g, *, seq, final, tm):
    b, _, d_inner = gact.shape
    d = w.shape[1]
    rows = seq if final else xa.shape[1]
    split = gctx is not None
    if split:
        assert seq % tm == 0 and gctx.shape[1] == tm
        last_lat = seq // tm - 1
        g_map = lambda bi, i: (bi, jnp.minimum(i, last_lat), 0)
    else:
        gctx = gact
        g_map = lambda bi, i: (bi, i, 0)
    kern = functools.partial(_out_kernel, tm=tm, seq=seq, final=final, split=split)
    return pl.pallas_call(
        kern,
        grid=(b, rows // tm),
        in_specs=[
            pl.BlockSpec((1, tm, d_inner), g_map),
            pl.BlockSpec((1, tm, d_inner), lambda bi, i: (bi, 0, 0)),
            pl.BlockSpec((d_inner, d), lambda bi, i: (0, 0)),
            pl.BlockSpec((1, tm, d), lambda bi, i: (bi, i, 0)),
            pl.BlockSpec((1, 8, d), lambda bi, i: (bi, 0, 0)),
            pl.BlockSpec((1, d), lambda bi, i: (0, 0)),
        ],
        out_specs=pl.BlockSpec((1, tm, d), lambda bi, i: (bi, i, 0)),
        out_shape=jax.ShapeDtypeStruct((b, rows, d), F32),
        compiler_params=_params(("parallel", "parallel"), 48),
        name="out_proj",
    )(gact, gctx, w, xa, mod, final_g)


def kernel(x, c, ctx, c_ctx, norm_g, w_ada, b_ada, a_w_in, a_w_out, a_lam_q1, a_lam_k1, a_lam_q2, a_lam_k2, a_subln_g, b_w_in, b_w_grp, b_b_grp, b_scale, b_w_out, c_w_in, c_sink, c_w_out, final_g):
    b, seq, d = x.shape
    n_ctx = ctx.shape[1]
    depth = w_ada.shape[0]
    xa = jnp.concatenate([x, ctx], axis=1)

    cond8 = jnp.concatenate([c, c_ctx[None, :], jnp.zeros((8 - b - 1, d), F32)], axis=0)
    ada = _adaln(cond8, w_ada, b_ada)

    cos_a, sin_a = _rope_tables(seq, n_ctx, DA_HD)
    cos_c, sin_c = _rope_tables(seq, n_ctx, WC_HD)

    out = None
    for i in range(depth):
        m = i % N_MIXERS
        j = i // N_MIXERS
        last = i == depth - 1
        lat = ada[i, :b].reshape(b, 3, d)
        cm = jnp.broadcast_to(ada[i, b].reshape(1, 3, d), (b, 3, d))
        mod = jnp.concatenate([lat, cm, jnp.zeros((b, 2, d), F32)], axis=1)
        g = norm_g[i].reshape(1, d)
        gctx = None

        if m == 0:
            lam_init = 0.8 - 0.6 * math.exp(-0.3 * i)
            n_qk = DA_HEADS * 2 * DA_HD
            w_in = _permute_rope_columns(a_w_in[j], 2 * n_qk, DA_HD).astype(BF16)
            qkvz = _proj(xa, mod, g, w_in, cos_a, sin_a, seq=seq, n_q=n_qk, n_k=n_qk,
                         qscale=DA_HD ** -0.5 * LOG2E)
            lamp = jnp.stack([a_lam_q1[j], a_lam_k1[j], a_lam_q2[j], a_lam_k2[j]], axis=0)
            sg = a_subln_g[j].reshape(1, DA_VD)
            gact = _diff_attn(qkvz, lamp, sg, seq=seq, n_ctx=n_ctx, lam_init=lam_init,
                              ctx_queries=False)
            if not last:
                gctx = _diff_attn(qkvz, lamp, sg, seq=seq, n_ctx=n_ctx, lam_init=lam_init,
                                  ctx_queries=True)
            w_out = a_w_out[j]
        elif m == 1:
            gact = _pool_mixer(xa, mod, g, b_w_in[j].astype(BF16), b_w_grp[j].astype(BF16),
                               b_b_grp[j].reshape(1, -1), b_scale[j].reshape(1, -1),
                               seq=seq, n_ctx=n_ctx)
            w_out = b_w_out[j]
        else:
            n_q, n_k = WC_KV * WC_G * WC_HD, WC_KV * WC_HD
            w_in = _permute_rope_columns(c_w_in[j], n_q + n_k, WC_HD).astype(BF16)
            qkvz = _proj(xa, mod, g, w_in, cos_c, sin_c, seq=seq, n_q=n_q, n_k=n_k,
                         qscale=WC_HD ** -0.5 * LOG2E)
            tq = 256
            sink_rows = jnp.repeat(c_sink[j].astype(F32) * LOG2E, tq).reshape(WC_KV, 1, WC_G * tq)
            gact = _swa_attn(qkvz, sink_rows, seq=seq, n_ctx=n_ctx, tq=tq)
            w_out = c_w_out[j]

        if last:
            tm = 512
        else:
            tm = 544 if gctx is None else n_ctx
        res = _out_proj(gact, gctx, w_out.astype(BF16), xa, mod, final_g.reshape(1, d), seq=seq,
                        final=last, tm=tm)
        if last:
            out = res
        else:
            xa = res
    return out
```

```python
import functools
import math

import jax
import jax.numpy as jnp
from jax import lax
from jax.experimental import pallas as pl
from jax.experimental.pallas import tpu as pltpu

F32 = jnp.float32
BF16 = jnp.bfloat16

GRID_W = 64
EPS = 1e-6
ROPE_BASE = 10000.0
N_MIXERS = 3
DA_HEADS = 16
DA_HD = 64
DA_VD = 128
POOL_WINDOWS = (2, 4, 8, 16)
POOL_GW = 512
WC_HD = 128
WC_KV = 4
WC_G = 4
WINDOW = 128

LOG2E = 1.4426950408889634
LANE = 128
MIB = 1024 * 1024


def _params(semantics, vmem_mib):
    return pltpu.CompilerParams(dimension_semantics=semantics,
                                vmem_limit_bytes=vmem_mib * MIB)


def _silu(v):
    return v * jax.nn.sigmoid(v)


def _adaln_kernel(c_ref, w_ref, b_ref, o_ref):
    a = _silu(c_ref[...]).astype(BF16)
    o_ref[0] = jnp.dot(a, w_ref[0].astype(BF16), preferred_element_type=F32) + b_ref[0]


def _adaln(cond8, w_ada, b_ada):
    depth, d, d3 = w_ada.shape
    nt = d3 // d
    return pl.pallas_call(
        _adaln_kernel,
        grid=(depth, nt),
        in_specs=[
            pl.BlockSpec((8, d), lambda l, n: (0, 0)),
            pl.BlockSpec((1, d, d), lambda l, n: (l, 0, n)),
            pl.BlockSpec((1, 1, d), lambda l, n: (l, 0, n)),
        ],
        out_specs=pl.BlockSpec((1, 8, d), lambda l, n: (l, 0, n)),
        out_shape=jax.ShapeDtypeStruct((depth, 8, d3), F32),
        compiler_params=_params(("parallel", "parallel"), 32),
        name="adaln",
    )(cond8, w_ada, b_ada.reshape(depth, 1, d3))


def _norm_mod(xf, g, shift, scale):
    y = xf * lax.rsqrt(jnp.mean(xf * xf, axis=-1, keepdims=True) + EPS) * g
    return y * (1.0 + scale) + shift


SUB = 256
HALF = LANE // 2


def _rope_lane_layout(head_dim):
    quarter = head_dim // 4
    lane = jnp.arange(LANE)
    second = lane // HALF
    grp = (lane % HALF) // quarter
    freq = lane % quarter
    axis = grp % 2
    unit = grp // 2
    old = unit * head_dim + (axis * 2 + second) * quarter + freq
    sign = jnp.where(second == 0, -1.0, 1.0).astype(F32)
    return old, axis, freq, sign


def _rope_tables(seq, n_ctx, head_dim):
    quarter = head_dim // 4
    _, axis, freq, sign = _rope_lane_layout(head_dim)
    inv = ROPE_BASE ** (-jnp.arange(quarter, dtype=F32) / quarter)
    pos = jnp.arange(seq)
    coord = jnp.where(axis[None, :] == 0, (pos // GRID_W)[:, None], (pos % GRID_W)[:, None])
    ang = coord.astype(F32) * inv[freq][None, :]
    cos = jnp.concatenate([jnp.cos(ang), jnp.ones((n_ctx, LANE), F32)], axis=0)
    sin = jnp.concatenate([jnp.sin(ang) * sign[None, :], jnp.zeros((n_ctx, LANE), F32)], axis=0)
    return cos, sin


def _permute_rope_columns(w, n_cols, head_dim):
    quarter = head_dim // 4
    units = LANE // head_dim
    d = w.shape[0]
    head = w[:, :n_cols].reshape(d, n_cols // LANE, units, 2, 2, quarter)
    head = head.transpose(0, 1, 4, 2, 3, 5).reshape(d, n_cols)
    return jnp.concatenate([head, w[:, n_cols:]], axis=1)


def _proj_kernel(x_ref, mod_ref, g_ref, w_ref, cos_ref, sin_ref, o_ref, h_ref,
                 *, tm, seq, groups, qscale):
    i = pl.program_id(1)
    j = pl.program_id(2)

    @pl.when(j == 0)
    def _():
        row = i * tm + lax.broadcasted_iota(jnp.int32, (tm, 1), 0)
        is_ctx = row >= seq
        shift = jnp.where(is_ctx, mod_ref[0, 3:4, :], mod_ref[0, 0:1, :])
        scale = jnp.where(is_ctx, mod_ref[0, 4:5, :], mod_ref[0, 1:2, :])
        h_ref[...] = _norm_mod(x_ref[0], g_ref[...], shift, scale).astype(BF16)

    def emit(kinds):
        tables = {}
        for kind in set(kinds) - {"P"}:
            mul = qscale if kind == "Q" else 1.0
            tables[kind] = (cos_ref[...] * mul, sin_ref[...] * mul)
        for s, kind in enumerate(kinds):
            acc = jnp.dot(h_ref[...], w_ref[:, s * SUB:(s + 1) * SUB],
                          preferred_element_type=F32)
            if kind == "P":
                o_ref[0, :, s * SUB:(s + 1) * SUB] = acc.astype(o_ref.dtype)
                continue
            cos, sin = tables[kind]
            for u in range(SUB // LANE):
                t = acc[:, u * LANE:(u + 1) * LANE]
                c0 = s * SUB + u * LANE
                o_ref[0, :, c0:c0 + LANE] = (t * cos + pltpu.roll(t, HALF, 1) * sin
                                             ).astype(o_ref.dtype)

    for j_lo, j_hi, kinds in groups:
        pl.when(jnp.logical_and(j >= j_lo, j <= j_hi))(functools.partial(emit, kinds))


def _proj(xa, mod, g, w, cos, sin, *, seq, n_q, n_k, qscale, tm=1088, tn=1024):
    b, t, d = xa.shape
    n = w.shape[1]
    per_tile = tn // SUB
    kinds = ["Q"] * (n_q // SUB) + ["K"] * (n_k // SUB) + ["P"] * ((n - n_q - n_k) // SUB)
    tiles = [tuple(kinds[jt * per_tile:(jt + 1) * per_tile]) for jt in range(n // tn)]
    groups = []
    for jt, kt in enumerate(tiles):
        if groups and groups[-1][2] == kt:
            groups[-1] = (groups[-1][0], jt, kt)
        else:
            groups.append((jt, jt, kt))
    kern = functools.partial(_proj_kernel, tm=tm, seq=seq, groups=tuple(groups), qscale=qscale)
    return pl.pallas_call(
        kern,
        grid=(b, t // tm, n // tn),
        in_specs=[
            pl.BlockSpec((1, tm, d), lambda bi, i, j: (bi, i, 0)),
            pl.BlockSpec((1, 8, d), lambda bi, i, j: (bi, 0, 0)),
            pl.BlockSpec((1, d), lambda bi, i, j: (0, 0)),
            pl.BlockSpec((d, tn), lambda bi, i, j: (0, j)),
            pl.BlockSpec((tm, LANE), lambda bi, i, j: (i, 0)),
            pl.BlockSpec((tm, LANE), lambda bi, i, j: (i, 0)),
        ],
        out_specs=pl.BlockSpec((1, tm, tn), lambda bi, i, j: (bi, i, j)),
        out_shape=jax.ShapeDtypeStruct((b, t, n), BF16),
        scratch_shapes=[pltpu.VMEM((tm, d), BF16)],
        compiler_params=_params(("parallel", "parallel", "arbitrary"), 48),
        name="proj",
    )(xa, mod, g, w, cos, sin)


PEAK_LIMIT = 2.0 ** 64
ONES = 16


def _stage_vt(v_ref, vt_ref):
    rows, vd = v_ref.shape[1], v_ref.shape[2]
    vt_ref[0:vd, :] = v_ref[0].T
    vt_ref[vd:vd + ONES, :] = jnp.ones((ONES, rows), vt_ref.dtype)


def _diff_attn_kernel(lamp_ref, q_ref, k_ref, v_ref, z_ref, sg_ref, o_ref, acc_ref, vt_ref,
                      *, tq, nsub, tk, chunks, lam_init):
    @pl.when(pl.program_id(2) == 0)
    def _():
        _stage_vt(v_ref, vt_ref)

    lp = lamp_ref[...]
    lam = (jnp.exp(jnp.sum(lp[0:1] * lp[1:2], axis=1, keepdims=True))
           - jnp.exp(jnp.sum(lp[2:3] * lp[3:4], axis=1, keepdims=True)) + lam_init)

    def queries(u):
        q = q_ref[0, u * tq:(u + 1) * tq, :].astype(F32)
        lane = lax.broadcasted_iota(jnp.int32, q.shape, 1)
        map0 = lane % HALF < HALF // 2
        return jnp.concatenate([jnp.where(map0, q, 0.0),
                                jnp.where(map0, 0.0, q)], axis=0).astype(BF16)

    def scores(qm, c0, cl):
        return lax.dot_general(k_ref[0, pl.ds(c0, cl), :], qm, (((1,), (1,)), ((), ())),
                               preferred_element_type=F32)

    def pv(e, c0, cl):
        return jnp.dot(vt_ref[:, pl.ds(c0, cl)], e.astype(BF16),
                       preferred_element_type=F32)

    def finish(u):
        acc = acc_ref[u]
        inv = 1.0 / acc[DA_VD:DA_VD + 1, :]
        o_t = (acc[:DA_VD, :tq] * inv[:, :tq]
               - lam * (acc[:DA_VD, tq:] * inv[:, tq:]))
        o = o_t.T
        o = o * lax.rsqrt(jnp.mean(o * o, axis=-1, keepdims=True) + EPS) * sg_ref[...]
        o = o * (1.0 - lam_init)
        z = z_ref[0, u * tq:(u + 1) * tq, :].astype(F32)
        o_ref[0, u * tq:(u + 1) * tq, :] = (o * _silu(z)).astype(o_ref.dtype)

    def fast(u, qm):
        m = peak = None
        for c0, cl in chunks:
            s = scores(qm, c0, cl)
            if m is None:
                m = jnp.max(s[0:8], axis=0, keepdims=True)
            e = jnp.exp2(s - m).astype(BF16)
            emax = jnp.max(e, axis=0, keepdims=True).astype(F32)
            up = jnp.maximum(emax, 1.0)
            alpha = 1.0 / up
            if peak is None:
                acc_ref[u] = pv(e, c0, cl) * alpha
                peak = emax
            else:
                acc_ref[u] = (acc_ref[u] + pv(e, c0, cl)) * alpha
                peak = jnp.maximum(peak, emax)
            m = m + jnp.log2(up)
        finish(u)
        return jnp.max(peak)

    def exact(u, qm):
        s = scores(qm, *chunks[0])
        m = jnp.max(s, axis=0, keepdims=True)
        acc_ref[u] = pv(jnp.exp2(s - m), *chunks[0])

        def body(c, m):
            c0 = pl.multiple_of(chunks[1][0] + c * tk, tk)
            s = scores(qm, c0, tk)
            m_new = jnp.maximum(m, jnp.max(s, axis=0, keepdims=True))
            acc_ref[u] = jnp.exp2(m - m_new) * acc_ref[u] + pv(jnp.exp2(s - m_new), c0, tk)
            return m_new

        if len(chunks) > 1:
            lax.fori_loop(0, len(chunks) - 1, body, m)
        finish(u)

    qms = [queries(u) for u in range(nsub)]
    peak = functools.reduce(jnp.maximum, [fast(u, qms[u]) for u in range(nsub)])

    @pl.when(jnp.logical_not(peak <= PEAK_LIMIT))
    def _():
        for u in range(nsub):
            exact(u, qms[u])


def _diff_attn(qkvz, lamp, sg, *, seq, n_ctx, lam_init, ctx_queries, tq=512, nsub=4, tk=512):
    b, t, _ = qkvz.shape
    h = DA_HEADS
    if ctx_queries:
        tq, nsub, chunks = n_ctx, 1, ((0, n_ctx),)
        q_blk0, k_rows, k_blk = seq // n_ctx, n_ctx, seq // n_ctx
        out_rows = n_ctx
    else:
        chunks = ((seq, n_ctx),) + tuple((c, tk) for c in range(0, seq, tk))
        q_blk0, k_rows, k_blk = 0, t, 0
        out_rows = seq
    kern = functools.partial(_diff_attn_kernel, tq=tq, nsub=nsub, tk=tk, chunks=chunks,
                             lam_init=lam_init)
    tqs = tq * nsub
    return pl.pallas_call(
        kern,
        grid=(b, h, out_rows // tqs),
        in_specs=[
            pl.BlockSpec((4, DA_HD), lambda bi, hi, i: (0, 0)),
            pl.BlockSpec((1, tqs, LANE), lambda bi, hi, i: (bi, q_blk0 + i, hi)),
            pl.BlockSpec((1, k_rows, LANE), lambda bi, hi, i: (bi, k_blk, h + hi)),
            pl.BlockSpec((1, k_rows, LANE), lambda bi, hi, i: (bi, k_blk, 2 * h + hi)),
            pl.BlockSpec((1, tqs, LANE), lambda bi, hi, i: (bi, q_blk0 + i, 3 * h + hi)),
            pl.BlockSpec((1, DA_VD), lambda bi, hi, i: (0, 0)),
        ],
        out_specs=pl.BlockSpec((1, tqs, LANE), lambda bi, hi, i: (bi, i, hi)),
        out_shape=jax.ShapeDtypeStruct((b, out_rows, h * DA_VD), BF16),
        scratch_shapes=[pltpu.VMEM((nsub, DA_VD + ONES, 2 * tq), F32),
                        pltpu.VMEM((DA_VD + ONES, k_rows), BF16)],
        compiler_params=_params(("parallel", "parallel", "arbitrary"), 48),
        name="diff_attn_ctx" if ctx_queries else "diff_attn",
    )(lamp, qkvz, qkvz, qkvz, qkvz, sg)


def _swa_kernel(q_ref, k_ref, v_ref, z_ref, sink_ref, o_ref, vt_ref,
                *, tq, nh, seq, n_ctx, n_lat_tiles):
    i = pl.program_id(2)
    gw = WC_G * WC_HD

    @pl.when(i == 0)
    def _():
        for hh in range(nh):
            vt_ref[hh, 0:WC_HD, :] = v_ref[0, :, hh * WC_HD:(hh + 1) * WC_HD].T
            vt_ref[hh, WC_HD:WC_HD + ONES, :] = jnp.ones((ONES, v_ref.shape[1]), vt_ref.dtype)

    span = tq + 2 * WINDOW
    nt = (((1,), (1,)), ((), ()))

    def head(hh):
        q4 = jnp.concatenate([q_ref[0, :, hh * gw + g * WC_HD:hh * gw + (g + 1) * WC_HD]
                              for g in range(WC_G)], axis=0)
        k_cols = slice(hh * WC_HD, (hh + 1) * WC_HD)
        s_ctx = lax.dot_general(k_ref[0, seq:seq + n_ctx, k_cols], q4, nt,
                                preferred_element_type=F32)
        return q4, k_cols, s_ctx, sink_ref[hh]

    def finish(hh, o_t, den):
        o_t = o_t * (1.0 / den)
        for g in range(WC_G):
            cols = slice(hh * gw + g * WC_HD, hh * gw + (g + 1) * WC_HD)
            o = o_t[:, g * tq:(g + 1) * tq].T
            o_ref[0, :, cols] = (o * _silu(z_ref[0, :, cols].astype(F32))).astype(o_ref.dtype)

    def fast(hh, s_ctx, sink, s_lat, base):
        e_ctx = jnp.exp2(s_ctx).astype(BF16)
        acc = jnp.dot(vt_ref[hh, :, seq:seq + n_ctx], e_ctx, preferred_element_type=F32)
        peak = jnp.max(e_ctx, axis=0, keepdims=True).astype(F32)
        if s_lat is not None:
            e_lat = jnp.exp2(s_lat).astype(BF16)
            acc = acc + jnp.dot(vt_ref[hh, :, pl.ds(base, span)], e_lat,
                                preferred_element_type=F32)
            peak = jnp.maximum(peak, jnp.max(e_lat, axis=0, keepdims=True).astype(F32))
        e_sink = jnp.exp2(sink)
        den = acc[WC_HD:WC_HD + 1, :] + e_sink
        finish(hh, acc[:WC_HD, :], den)
        return jnp.logical_and(jnp.max(jnp.maximum(peak, e_sink)) <= PEAK_LIMIT,
                               jnp.min(den) >= 1.0 / PEAK_LIMIT)

    def exact(hh, s_ctx, sink, s_lat, base):
        m = jnp.maximum(jnp.max(s_ctx, axis=0, keepdims=True), sink)
        if s_lat is not None:
            m = jnp.maximum(m, jnp.max(s_lat, axis=0, keepdims=True))
        e_c = jnp.exp2(s_ctx - m)
        den = jnp.sum(e_c, axis=0, keepdims=True) + jnp.exp2(sink - m)
        o_t = jnp.dot(vt_ref[hh, :WC_HD, seq:seq + n_ctx], e_c.astype(BF16),
                      preferred_element_type=F32)
        if s_lat is not None:
            e_l = jnp.exp2(s_lat - m)
            den = den + jnp.sum(e_l, axis=0, keepdims=True)
            o_t = o_t + jnp.dot(vt_ref[hh, :WC_HD, pl.ds(base, span)], e_l.astype(BF16),
                                preferred_element_type=F32)
        finish(hh, o_t, den)

    def tile(latent):
        if latent:
            start = i * tq
            base = pl.multiple_of(jnp.clip(start - WINDOW, 0, seq - span), LANE)
            kpos = base + lax.broadcasted_iota(jnp.int32, (span, 1), 0)
            qpos = start + lax.broadcasted_iota(jnp.int32, (1, tq), 1)
            bias = jnp.where(jnp.abs(qpos - kpos) <= WINDOW, 0.0, -jnp.inf)
            bias = jnp.concatenate([bias] * WC_G, axis=1)
        args = []
        for hh in range(nh):
            q4, k_cols, s_ctx, sink = head(hh)
            s_lat = None
            if latent:
                s_lat = lax.dot_general(k_ref[0, pl.ds(base, span), k_cols], q4, nt,
                                        preferred_element_type=F32) + bias
            args.append((hh, s_ctx, sink, s_lat, base if latent else None))
        in_range = functools.reduce(jnp.logical_and, [fast(*a) for a in args])

        @pl.when(jnp.logical_not(in_range))
        def _():
            for a in args:
                exact(*a)

    pl.when(i < n_lat_tiles)(functools.partial(tile, True))
    pl.when(i >= n_lat_tiles)(functools.partial(tile, False))


def _swa_attn(qkvz, sink_rows, *, seq, n_ctx, tq=256, nh=2):
    b, t, _ = qkvz.shape
    gw = WC_G * WC_HD
    n_lat = seq // tq
    kern = functools.partial(_swa_kernel, tq=tq, nh=nh, seq=seq, n_ctx=n_ctx, n_lat_tiles=n_lat)
    k_col0, v_col0, z_col0 = WC_KV * gw, WC_KV * gw + WC_KV * WC_HD, WC_KV * gw + 2 * WC_KV * WC_HD
    assert k_col0 % (nh * WC_HD) == 0 and v_col0 % (nh * WC_HD) == 0 and z_col0 % (nh * gw) == 0
    k_blk0, v_blk0, z_blk0 = k_col0 // (nh * WC_HD), v_col0 // (nh * WC_HD), z_col0 // (nh * gw)
    return pl.pallas_call(
        kern,
        grid=(b, WC_KV // nh, t // tq),
        in_specs=[
            pl.BlockSpec((1, tq, nh * gw), lambda bi, n, i: (bi, i, n)),
            pl.BlockSpec((1, t, nh * WC_HD), lambda bi, n, i: (bi, 0, k_blk0 + n)),
            pl.BlockSpec((1, t, nh * WC_HD), lambda bi, n, i: (bi, 0, v_blk0 + n)),
            pl.BlockSpec((1, tq, nh * gw), lambda bi, n, i: (bi, i, z_blk0 + n)),
            pl.BlockSpec((nh, 1, WC_G * tq), lambda bi, n, i: (n, 0, 0)),
        ],
        out_specs=pl.BlockSpec((1, tq, nh * gw), lambda bi, n, i: (bi, i, n)),
        out_shape=jax.ShapeDtypeStruct((b, t, WC_KV * gw), BF16),
        scratch_shapes=[pltpu.VMEM((nh, WC_HD + ONES, t), BF16)],
        compiler_params=_params(("parallel", "parallel", "arbitrary"), 48),
        name="swa_attn",
    )(qkvz, qkvz, qkvz, qkvz, sink_rows)


HALO = 8


def _pool_kernel(x_ref, xp_ref, xn_ref, mod_ref, g_ref, w_ref, wg_ref, bg_ref, sc_ref,
                 o_ref, hb_ref, s2_ref, s4_ref, s8_ref, *, tm, seq, n_ctx, n_lat_tiles):
    i = pl.program_id(1)
    is_ctx = i >= n_lat_tiles
    shift = jnp.where(is_ctx, mod_ref[0, 3:4, :], mod_ref[0, 0:1, :])
    scale = jnp.where(is_ctx, mod_ref[0, 4:5, :], mod_ref[0, 1:2, :])
    g = g_ref[...]
    has_prev = jnp.logical_and(i != 0, i != n_lat_tiles)
    has_next = jnp.logical_and(i != n_lat_tiles - 1, i < n_lat_tiles)
    h = _norm_mod(x_ref[0], g, shift, scale)
    hb_ref[HALO:HALO + tm, :] = h
    hb_ref[0:HALO, :] = jnp.where(has_prev, _norm_mod(xp_ref[0], g, shift, scale), 0.0)
    hb_ref[HALO + tm:, :] = jnp.where(has_next, _norm_mod(xn_ref[0], g, shift, scale), 0.0)
    n = tm + 2 * HALO
    s2_ref[1:n, :] = hb_ref[0:n - 1, :] + hb_ref[1:n, :]
    s4_ref[2:n - 1, :] = s2_ref[1:n - 2, :] + s2_ref[3:n, :]
    s8_ref[4:n - 3, :] = s4_ref[2:n - 5, :] + s4_ref[6:n - 1, :]
    sums = (s2_ref[HALO:HALO + tm, :], s4_ref[HALO:HALO + tm, :], s8_ref[HALO:HALO + tm, :],
            s8_ref[HALO - 4:HALO - 4 + tm, :] + s8_ref[HALO + 4:HALO + 4 + tm, :])
    pos = jnp.where(is_ctx, 0, i * tm) + lax.broadcasted_iota(jnp.int32, (tm, 1), 0)
    seg_len = jnp.where(is_ctx, n_ctx, seq)
    hb16 = h.astype(BF16)
    d_inner = len(POOL_WINDOWS) * POOL_GW
    for gi, w in enumerate(POOL_WINDOWS):
        lo = jnp.maximum(pos - w // 2, 0)
        hi = jnp.minimum(pos - w // 2 + w, seg_len)
        cnt = (hi - lo).astype(F32)
        hd = (sums[gi] / cnt - h).astype(BF16)
        cols = slice(gi * POOL_GW, (gi + 1) * POOL_GW)
        d = jnp.dot(hd, w_ref[:, cols], preferred_element_type=F32)
        y = jnp.dot(d.astype(BF16), wg_ref[gi], preferred_element_type=F32) + bg_ref[:, cols]
        y = y * sc_ref[:, cols]
        z = jnp.dot(hb16, w_ref[:, d_inner + gi * POOL_GW:d_inner + (gi + 1) * POOL_GW],
                    preferred_element_type=F32)
        o_ref[0, :, cols] = (y * _silu(z)).astype(o_ref.dtype)


def _pool_mixer(xa, mod, g, w_in, w_grp, b_grp, scale, *, seq, n_ctx, tm=256):
    b, t, d = xa.shape
    d_inner = len(POOL_WINDOWS) * POOL_GW
    hb = tm // HALO
    last = t // HALO - 1
    kern = functools.partial(_pool_kernel, tm=tm, seq=seq, n_ctx=n_ctx, n_lat_tiles=seq // tm)
    buf = pltpu.VMEM((tm + 2 * HALO, d), F32)
    return pl.pallas_call(
        kern,
        grid=(b, t // tm),
        in_specs=[
            pl.BlockSpec((1, tm, d), lambda bi, i: (bi, i, 0)),
            pl.BlockSpec((1, HALO, d), lambda bi, i: (bi, jnp.maximum(i * hb - 1, 0), 0)),
            pl.BlockSpec((1, HALO, d), lambda bi, i: (bi, jnp.minimum((i + 1) * hb, last), 0)),
            pl.BlockSpec((1, 8, d), lambda bi, i: (bi, 0, 0)),
            pl.BlockSpec((1, d), lambda bi, i: (0, 0)),
            pl.BlockSpec((d, 2 * d_inner), lambda bi, i: (0, 0)),
            pl.BlockSpec((len(POOL_WINDOWS), POOL_GW, POOL_GW), lambda bi, i: (0, 0, 0)),
            pl.BlockSpec((1, d_inner), lambda bi, i: (0, 0)),
            pl.BlockSpec((1, d_inner), lambda bi, i: (0, 0)),
        ],
        out_specs=pl.BlockSpec((1, tm, d_inner), lambda bi, i: (bi, i, 0)),
        out_shape=jax.ShapeDtypeStruct((b, t, d_inner), BF16),
        scratch_shapes=[buf, buf, buf, buf],
        compiler_params=_params(("parallel", "arbitrary"), 56),
        name="pool_mixer",
    )(xa, xa, xa, mod, g, w_in, w_grp, b_grp, scale)


def _out_kernel(g_ref, gc_ref, w_ref, x_ref, mod_ref, fg_ref, o_ref, *, tm, seq, final, split):
    i = pl.program_id(1)
    is_ctx = i * tm >= seq

    def emit(g, gate):
        y = jnp.dot(g, w_ref[...], preferred_element_type=F32)
        xn = x_ref[0] + gate * y
        if final:
            xn = xn * lax.rsqrt(jnp.mean(xn * xn, axis=-1, keepdims=True) + EPS) * fg_ref[...]
        o_ref[0] = xn

    if split:
        @pl.when(jnp.logical_not(is_ctx))
        def _():
            emit(g_ref[0], mod_ref[0, 2:3, :])

        @pl.when(is_ctx)
        def _():
            emit(gc_ref[0], mod_ref[0, 5:6, :])
    else:
        row = i * tm + lax.broadcasted_iota(jnp.int32, (tm, 1), 0)
        emit(g_ref[0], jnp.where(row >= seq, mod_ref[0, 5:6, :], mod_ref[0, 2:3, :]))


def _out_proj(gact, gctx, w, xa, mod, final_g, *, seq, final, tm):
    b, _, d_inner = gact.shape
    d = w.shape[1]
    rows = seq if final else xa.shape[1]
    split = gctx is not None
    if split:
        assert seq % tm == 0 and gctx.shape[1] == tm
        last_lat = seq // tm - 1
        g_map = lambda bi, i: (bi, jnp.minimum(i, last_lat), 0)
    else:
        gctx = gact
        g_map = lambda bi, i: (bi, i, 0)
    kern = functools.partial(_out_kernel, tm=tm, seq=seq, final=final, split=split)
    return pl.pallas_call(
        kern,
        grid=(b, rows // tm),
        in_specs=[
            pl.BlockSpec((1, tm, d_inner), g_map),
            pl.BlockSpec((1, tm, d_inner), lambda bi, i: (bi, 0, 0)),
            pl.BlockSpec((d_inner, d), lambda bi, i: (0, 0)),
            pl.BlockSpec((1, tm, d), lambda bi, i: (bi, i, 0)),
            pl.BlockSpec((1, 8, d), lambda bi, i: (bi, 0, 0)),
            pl.BlockSpec((1, d), lambda bi, i: (0, 0)),
        ],
        out_specs=pl.BlockSpec((1, tm, d), lambda bi, i: (bi, i, 0)),
        out_shape=jax.ShapeDtypeStruct((b, rows, d), F32),
        compiler_params=_params(("parallel", "parallel"), 48),
        name="out_proj",
    )(gact, gctx, w, xa, mod, final_g)


def kernel(x, c, ctx, c_ctx, norm_g, w_ada, b_ada, a_w_in, a_w_out, a_lam_q1, a_lam_k1, a_lam_q2, a_lam_k2, a_subln_g, b_w_in, b_w_grp, b_b_grp, b_scale, b_w_out, c_w_in, c_sink, c_w_out, final_g):
    b, seq, d = x.shape
    n_ctx = ctx.shape[1]
    depth = w_ada.shape[0]
    xa = jnp.concatenate([x, ctx], axis=1)

    cond8 = jnp.concatenate([c, c_ctx[None, :], jnp.zeros((8 - b - 1, d), F32)], axis=0)
    ada = _adaln(cond8, w_ada, b_ada)

    cos_a, sin_a = _rope_tables(seq, n_ctx, DA_HD)
    cos_c, sin_c = _rope_tables(seq, n_ctx, WC_HD)

    out = None
    for i in range(depth):
        m = i % N_MIXERS
        j = i // N_MIXERS
        last = i == depth - 1
        lat = ada[i, :b].reshape(b, 3, d)
        cm = jnp.broadcast_to(ada[i, b].reshape(1, 3, d), (b, 3, d))
        mod = jnp.concatenate([lat, cm, jnp.zeros((b, 2, d), F32)], axis=1)
        g = norm_g[i].reshape(1, d)
        gctx = None

        if m == 0:
            lam_init = 0.8 - 0.6 * math.exp(-0.3 * i)
            n_qk = DA_HEADS * 2 * DA_HD
            w_in = _permute_rope_columns(a_w_in[j], 2 * n_qk, DA_HD).astype(BF16)
            qkvz = _proj(xa, mod, g, w_in, cos_a, sin_a, seq=seq, n_q=n_qk, n_k=n_qk,
                         qscale=DA_HD ** -0.5 * LOG2E)
            lamp = jnp.stack([a_lam_q1[j], a_lam_k1[j], a_lam_q2[j], a_lam_k2[j]], axis=0)
            sg = a_subln_g[j].reshape(1, DA_VD)
            gact = _diff_attn(qkvz, lamp, sg, seq=seq, n_ctx=n_ctx, lam_init=lam_init,
                              ctx_queries=False)
            if not last:
                gctx = _diff_attn(qkvz, lamp, sg, seq=seq, n_ctx=n_ctx, lam_init=lam_init,
                                  ctx_queries=True)
            w_out = a_w_out[j]
        elif m == 1:
            gact = _pool_mixer(xa, mod, g, b_w_in[j].astype(BF16), b_w_grp[j].astype(BF16),
                               b_b_grp[j].reshape(1, -1), b_scale[j].reshape(1, -1),
                               seq=seq, n_ctx=n_ctx)
            w_out = b_w_out[j]
        else:
            n_q, n_k = WC_KV * WC_G * WC_HD, WC_KV * WC_HD
            w_in = _permute_rope_columns(c_w_in[j], n_q + n_k, WC_HD).astype(BF16)
            qkvz = _proj(xa, mod, g, w_in, cos_c, sin_c, seq=seq, n_q=n_q, n_k=n_k,
                         qscale=WC_HD ** -0.5 * LOG2E)
            tq = 256
            sink_rows = jnp.repeat(c_sink[j].astype(F32) * LOG2E, tq).reshape(WC_KV, 1, WC_G * tq)
            gact = _swa_attn(qkvz, sink_rows, seq=seq, n_ctx=n_ctx, tq=tq)
            w_out = c_w_out[j]

        if last:
            tm = 512
        else:
            tm = 544 if gctx is None else n_ctx
        res = _out_proj(gact, gctx, w_out.astype(BF16), xa, mod, final_g.reshape(1, d), seq=seq,
                        final=last, tm=tm)
        if last:
            out = res
        else:
            xa = res
    return out
```

```python
import functools
import math

import jax
import jax.numpy as jnp
from jax import lax
from jax.experimental import pallas as pl
from jax.experimental.pallas import tpu as pltpu

F32 = jnp.float32
BF16 = jnp.bfloat16

GRID_W = 64
EPS = 1e-6
ROPE_BASE = 10000.0
N_MIXERS = 3
DA_HEADS = 16
DA_HD = 64
DA_VD = 128
POOL_WINDOWS = (2, 4, 8, 16)
POOL_GW = 512
WC_HD = 128
WC_KV = 4
WC_G = 4
WINDOW = 128

LOG2E = 1.4426950408889634
LANE = 128
SUB = 256
V7X_VMEM_BYTES = 64 * 1024 * 1024

PROJ_TM, PROJ_TN = 1088, 1024
OUT_TM = 544
OUT_TM_FINAL = 512
POOL_TM = 256
ATTN_TQ, ATTN_TK = 512, 512
ATTN_NSUB = 4
SWA_TQ, SWA_NH = 256, 4


def _params(semantics, vmem_bytes):
    assert vmem_bytes <= V7X_VMEM_BYTES
    return pltpu.CompilerParams(dimension_semantics=semantics, vmem_limit_bytes=int(vmem_bytes))


def _silu(v):
    return v * jax.nn.sigmoid(v)


def _adaln_kernel(c_ref, w_ref, b_ref, o_ref):
    a = _silu(c_ref[...]).astype(BF16)
    o_ref[0] = jnp.dot(a, w_ref[0].astype(BF16), preferred_element_type=F32) + b_ref[0]


def _adaln(cond8, w_ada, b_ada):
    depth, d, d3 = w_ada.shape
    nt = d3 // d
    return pl.pallas_call(
        _adaln_kernel,
        grid=(depth, nt),
        in_specs=[
            pl.BlockSpec((8, d), lambda l, n: (0, 0)),
            pl.BlockSpec((1, d, d), lambda l, n: (l, 0, n)),
            pl.BlockSpec((1, 1, d), lambda l, n: (l, 0, n)),
        ],
        out_specs=pl.BlockSpec((1, 8, d), lambda l, n: (l, 0, n)),
        out_shape=jax.ShapeDtypeStruct((depth, 8, d3), F32),
        compiler_params=_params(("parallel", "parallel"), 2 * (d * d * 4 + 18 * d * 4) + d * d * 2),
        name="adaln",
    )(cond8, w_ada, b_ada.reshape(depth, 1, d3))


def _norm_mod(xf, g, shift, scale):
    inv = lax.rsqrt(jnp.mean(xf * xf, axis=-1, keepdims=True) + EPS)
    return xf * inv * (g * (1.0 + scale)) + shift


HALF = LANE // 2


def _rope_lane_layout(head_dim):
    quarter = head_dim // 4
    lane = jnp.arange(LANE)
    second = lane // HALF
    grp = (lane % HALF) // quarter
    freq = lane % quarter
    axis = grp % 2
    unit = grp // 2
    old = unit * head_dim + (axis * 2 + second) * quarter + freq
    sign = jnp.where(second == 0, -1.0, 1.0).astype(F32)
    return old, axis, freq, sign


def _rope_tables(seq, n_ctx, head_dim):
    quarter = head_dim // 4
    _, axis, freq, sign = _rope_lane_layout(head_dim)
    inv = ROPE_BASE ** (-jnp.arange(quarter, dtype=F32) / quarter)
    pos = jnp.arange(seq)
    coord = jnp.where(axis[None, :] == 0, (pos // GRID_W)[:, None], (pos % GRID_W)[:, None])
    ang = coord.astype(F32) * inv[freq][None, :]
    cos = jnp.concatenate([jnp.cos(ang), jnp.ones((n_ctx, LANE), F32)], axis=0)
    sin = jnp.concatenate([jnp.sin(ang) * sign[None, :], jnp.zeros((n_ctx, LANE), F32)], axis=0)
    return cos, sin


def _permute_rope_columns(w, n_cols, head_dim):
    quarter = head_dim // 4
    units = LANE // head_dim
    d = w.shape[0]
    head = w[:, :n_cols].reshape(d, n_cols // LANE, units, 2, 2, quarter)
    head = head.transpose(0, 1, 4, 2, 3, 5).reshape(d, n_cols)
    return jnp.concatenate([head, w[:, n_cols:]], axis=1)


def _proj_kernel(x_ref, mod_ref, g_ref, w_ref, cos_ref, sin_ref, o_ref, h_ref,
                 *, tm, seq, groups, qscale):
    i = pl.program_id(1)
    j = pl.program_id(2)

    def normalise(shift, scale):
        h_ref[...] = _norm_mod(x_ref[0], g_ref[...], shift, scale).astype(BF16)

    @pl.when(j == 0)
    def _():
        all_latent = (i + 1) * tm <= seq
        all_context = i * tm >= seq

        @pl.when(all_latent)
        def _():
            normalise(mod_ref[0, 0:1, :], mod_ref[0, 1:2, :])

        @pl.when(all_context)
        def _():
            normalise(mod_ref[0, 3:4, :], mod_ref[0, 4:5, :])

        @pl.when(jnp.logical_not(jnp.logical_or(all_latent, all_context)))
        def _():
            is_ctx = i * tm + lax.broadcasted_iota(jnp.int32, (tm, 1), 0) >= seq
            normalise(jnp.where(is_ctx, mod_ref[0, 3:4, :], mod_ref[0, 0:1, :]),
                      jnp.where(is_ctx, mod_ref[0, 4:5, :], mod_ref[0, 1:2, :]))

    def emit(kinds):
        tables = {}
        for kind in set(kinds) - {"P"}:
            mul = qscale if kind == "Q" else 1.0
            tables[kind] = (cos_ref[...] * mul, sin_ref[...] * mul)
        for s, kind in enumerate(kinds):
            acc = jnp.dot(h_ref[...], w_ref[:, s * SUB:(s + 1) * SUB],
                          preferred_element_type=F32)
            if kind == "P":
                o_ref[0, :, s * SUB:(s + 1) * SUB] = acc.astype(o_ref.dtype)
                continue
            cos, sin = tables[kind]
            for u in range(SUB // LANE):
                t = acc[:, u * LANE:(u + 1) * LANE]
                c0 = s * SUB + u * LANE
                o_ref[0, :, c0:c0 + LANE] = (t * cos + pltpu.roll(t, HALF, 1) * sin
                                             ).astype(o_ref.dtype)

    for j_lo, j_hi, kinds in groups:
        pl.when(jnp.logical_and(j >= j_lo, j <= j_hi))(functools.partial(emit, kinds))


def _proj(xa, mod, g, w, cos, sin, *, seq, sections, qscale, tm=PROJ_TM, tn=PROJ_TN):
    b, t, d = xa.shape
    n = w.shape[1]
    per_tile = tn // SUB
    kinds = [kind for kind, cols in sections for _ in range(cols // SUB)]
    assert len(kinds) * SUB == n
    tiles = [tuple(kinds[jt * per_tile:(jt + 1) * per_tile]) for jt in range(n // tn)]
    groups = []
    for jt, kt in enumerate(tiles):
        if groups and groups[-1][2] == kt:
            groups[-1] = (groups[-1][0], jt, kt)
        else:
            groups.append((jt, jt, kt))
    kern = functools.partial(_proj_kernel, tm=tm, seq=seq, groups=tuple(groups), qscale=qscale)
    return pl.pallas_call(
        kern,
        grid=(b, t // tm, n // tn),
        in_specs=[
            pl.BlockSpec((1, tm, d), lambda bi, i, j: (bi, i, 0)),
            pl.BlockSpec((1, 8, d), lambda bi, i, j: (bi, 0, 0)),
            pl.BlockSpec((1, d), lambda bi, i, j: (0, 0)),
            pl.BlockSpec((d, tn), lambda bi, i, j: (0, j)),
            pl.BlockSpec((tm, LANE), lambda bi, i, j: (i, 0)),
            pl.BlockSpec((tm, LANE), lambda bi, i, j: (i, 0)),
        ],
        out_specs=pl.BlockSpec((1, tm, tn), lambda bi, i, j: (bi, i, j)),
        out_shape=jax.ShapeDtypeStruct((b, t, n), BF16),
        scratch_shapes=[pltpu.VMEM((tm, d), BF16)],
        compiler_params=_params(("parallel", "parallel", "arbitrary"),
                                2 * (tm * d * 4 + d * tn * 2 + tm * tn * 2) + tm * d * 2
                                + 2 * tm * SUB * 4),
        name="proj",
    )(xa, mod, g, w, cos, sin)


PEAK_LIMIT = 2.0 ** 64
ONES = 16


def _stage_vt(v_ref, vt_ref):
    rows, vd = v_ref.shape[1], v_ref.shape[2]
    vt_ref[0:vd, :] = v_ref[0].T
    vt_ref[vd:vd + ONES, :] = jnp.ones((ONES, rows), vt_ref.dtype)


def _diff_attn_kernel(lamp_ref, q_ref, k_ref, v_ref, z_ref, sg_ref, o_ref, acc_ref, vt_ref,
                      *, tq, nsub, tk, chunks, lam_init):
    @pl.when(pl.program_id(2) == 0)
    def _():
        _stage_vt(v_ref, vt_ref)

    lp = lamp_ref[...]
    lam = (jnp.exp(jnp.sum(lp[0:1] * lp[1:2], axis=1, keepdims=True))
           - jnp.exp(jnp.sum(lp[2:3] * lp[3:4], axis=1, keepdims=True)) + lam_init)

    def queries(u):
        q = q_ref[0, u * tq:(u + 1) * tq, :].astype(F32)
        lane = lax.broadcasted_iota(jnp.int32, q.shape, 1)
        map0 = lane % HALF < HALF // 2
        return jnp.concatenate([jnp.where(map0, q, 0.0),
                                jnp.where(map0, 0.0, q)], axis=0).astype(BF16)

    def scores(qm, c0, cl):
        return lax.dot_general(k_ref[0, pl.ds(c0, cl), :], qm, (((1,), (1,)), ((), ())),
                               preferred_element_type=F32)

    def pv(e, c0, cl):
        return jnp.dot(vt_ref[:, pl.ds(c0, cl)], e.astype(BF16),
                       preferred_element_type=F32)

    def finish(u):
        acc = acc_ref[u]
        inv = 1.0 / acc[DA_VD:DA_VD + 1, :]
        o_t = (acc[:DA_VD, :tq] * inv[:, :tq]
               - lam * (acc[:DA_VD, tq:] * inv[:, tq:]))
        o = o_t.T
        o = o * lax.rsqrt(jnp.mean(o * o, axis=-1, keepdims=True) + EPS) * sg_ref[...]
        o = o * (1.0 - lam_init)
        z = z_ref[0, u * tq:(u + 1) * tq, :].astype(F32)
        o_ref[0, u * tq:(u + 1) * tq, :] = (o * _silu(z)).astype(o_ref.dtype)

    def fast(u, qm):
        m = peak = None
        for c0, cl in chunks:
            s = scores(qm, c0, cl)
            if m is None:
                m = jnp.max(s[0:8], axis=0, keepdims=True)
            e = jnp.exp2(s - m).astype(BF16)
            emax = jnp.max(e, axis=0, keepdims=True).astype(F32)
            up = jnp.maximum(emax, 1.0)
            alpha = 1.0 / up
            if peak is None:
                acc_ref[u] = pv(e, c0, cl) * alpha
                peak = emax
            else:
                acc_ref[u] = (acc_ref[u] + pv(e, c0, cl)) * alpha
                peak = jnp.maximum(peak, emax)
            m = m + jnp.log2(up)
        finish(u)
        return jnp.max(peak)

    def exact(u, qm):
        s = scores(qm, *chunks[0])
        m = jnp.max(s, axis=0, keepdims=True)
        acc_ref[u] = pv(jnp.exp2(s - m), *chunks[0])

        def body(c, m):
            c0 = pl.multiple_of(chunks[1][0] + c * tk, tk)
            s = scores(qm, c0, tk)
            m_new = jnp.maximum(m, jnp.max(s, axis=0, keepdims=True))
            acc_ref[u] = jnp.exp2(m - m_new) * acc_ref[u] + pv(jnp.exp2(s - m_new), c0, tk)
            return m_new

        if len(chunks) > 1:
            lax.fori_loop(0, len(chunks) - 1, body, m)
        finish(u)

    qms = [queries(u) for u in range(nsub)]
    peak = functools.reduce(jnp.maximum, [fast(u, qms[u]) for u in range(nsub)])

    @pl.when(jnp.logical_not(peak <= PEAK_LIMIT))
    def _():
        for u in range(nsub):
            exact(u, qms[u])


def _diff_attn(qkvz, lamp, sg, *, seq, n_ctx, lam_init, ctx_queries,
               tq=ATTN_TQ, nsub=ATTN_NSUB, tk=ATTN_TK):
    b, t, _ = qkvz.shape
    h = DA_HEADS
    if ctx_queries:
        tq, nsub, chunks = n_ctx, 1, ((0, n_ctx),)
        q_blk0, k_rows, k_blk = seq // n_ctx, n_ctx, seq // n_ctx
        out_rows = n_ctx
    else:
        chunks = ((seq, n_ctx),) + tuple((c, tk) for c in range(0, seq, tk))
        q_blk0, k_rows, k_blk = 0, t, 0
        out_rows = seq
    kern = functools.partial(_diff_attn_kernel, tq=tq, nsub=nsub, tk=tk, chunks=chunks,
                             lam_init=lam_init)
    tqs = tq * nsub
    return pl.pallas_call(
        kern,
        grid=(b, h, out_rows // tqs),
        in_specs=[
            pl.BlockSpec((4, DA_HD), lambda bi, hi, i: (0, 0)),
            pl.BlockSpec((1, tqs, LANE), lambda bi, hi, i: (bi, q_blk0 + i, hi)),
            pl.BlockSpec((1, k_rows, LANE), lambda bi, hi, i: (bi, k_blk, h + hi)),
            pl.BlockSpec((1, k_rows, LANE), lambda bi, hi, i: (bi, k_blk, 2 * h + hi)),
            pl.BlockSpec((1, tqs, LANE), lambda bi, hi, i: (bi, q_blk0 + i, 3 * h + hi)),
            pl.BlockSpec((1, DA_VD), lambda bi, hi, i: (0, 0)),
        ],
        out_specs=pl.BlockSpec((1, tqs, LANE), lambda bi, hi, i: (bi, i, hi)),
        out_shape=jax.ShapeDtypeStruct((b, out_rows, h * DA_VD), BF16),
        scratch_shapes=[pltpu.VMEM((nsub, DA_VD + ONES, 2 * tq), F32),
                        pltpu.VMEM((DA_VD + ONES, k_rows), BF16)],
        compiler_params=_params(
            ("parallel", "parallel", "arbitrary"),
            2 * LANE * 2 * (3 * tqs + 2 * k_rows)
            + (DA_VD + ONES) * (nsub * 2 * tq * 4 + k_rows * 2)
            + 4 * tk * 2 * tq * 4),
        name="diff_attn_ctx" if ctx_queries else "diff_attn",
    )(lamp, qkvz, qkvz, qkvz, qkvz, sg)


def _swa_kernel(q_ref, k_ref, v_ref, z_ref, sink_ref, o_ref, vt_ref,
                *, tq, nh, seq, n_ctx, n_lat_tiles):
    i = pl.program_id(2)
    gw = WC_G * WC_HD

    @pl.when(i == 0)
    def _():
        for hh in range(nh):
            vt_ref[hh, 0:WC_HD, :] = v_ref[0, :, hh * WC_HD:(hh + 1) * WC_HD].T
            vt_ref[hh, WC_HD:WC_HD + ONES, :] = jnp.ones((ONES, v_ref.shape[1]), vt_ref.dtype)

    span = tq + 2 * WINDOW
    nt = (((1,), (1,)), ((), ()))

    def head(hh):
        q4 = jnp.concatenate([q_ref[0, :, hh * gw + g * WC_HD:hh * gw + (g + 1) * WC_HD]
                              for g in range(WC_G)], axis=0)
        k_cols = slice(hh * WC_HD, (hh + 1) * WC_HD)
        s_ctx = lax.dot_general(k_ref[0, seq:seq + n_ctx, k_cols], q4, nt,
                                preferred_element_type=F32)
        return q4, k_cols, s_ctx, sink_ref[hh]

    def finish(hh, o_t, den):
        o_t = o_t * (1.0 / den)
        for g in range(WC_G):
            cols = slice(hh * gw + g * WC_HD, hh * gw + (g + 1) * WC_HD)
            o = o_t[:, g * tq:(g + 1) * tq].T
            o_ref[0, :, cols] = (o * _silu(z_ref[0, :, cols].astype(F32))).astype(o_ref.dtype)

    def fast(hh, s_ctx, sink, s_lat, base):
        e_ctx = jnp.exp2(s_ctx).astype(BF16)
        acc = jnp.dot(vt_ref[hh, :, seq:seq + n_ctx], e_ctx, preferred_element_type=F32)
        peak = jnp.max(e_ctx, axis=0, keepdims=True).astype(F32)
        if s_lat is not None:
            e_lat = jnp.exp2(s_lat).astype(BF16)
            acc = acc + jnp.dot(vt_ref[hh, :, pl.ds(base, span)], e_lat,
                                preferred_element_type=F32)
            peak = jnp.maximum(peak, jnp.max(e_lat, axis=0, keepdims=True).astype(F32))
        e_sink = jnp.exp2(sink)
        den = acc[WC_HD:WC_HD + 1, :] + e_sink
        finish(hh, acc[:WC_HD, :], den)
        return jnp.logical_and(jnp.max(jnp.maximum(peak, e_sink)) <= PEAK_LIMIT,
                               jnp.min(den) >= 1.0 / PEAK_LIMIT)

    def exact(hh, s_ctx, sink, s_lat, base):
        m = jnp.maximum(jnp.max(s_ctx, axis=0, keepdims=True), sink)
        if s_lat is not None:
            m = jnp.maximum(m, jnp.max(s_lat, axis=0, keepdims=True))
        e_c = jnp.exp2(s_ctx - m)
        den = jnp.sum(e_c, axis=0, keepdims=True) + jnp.exp2(sink - m)
        o_t = jnp.dot(vt_ref[hh, :WC_HD, seq:seq + n_ctx], e_c.astype(BF16),
                      preferred_element_type=F32)
        if s_lat is not None:
            e_l = jnp.exp2(s_lat - m)
            den = den + jnp.sum(e_l, axis=0, keepdims=True)
            o_t = o_t + jnp.dot(vt_ref[hh, :WC_HD, pl.ds(base, span)], e_l.astype(BF16),
                                preferred_element_type=F32)
        finish(hh, o_t, den)

    def tile(latent):
        if latent:
            start = i * tq
            base = pl.multiple_of(jnp.clip(start - WINDOW, 0, seq - span), LANE)
            kpos = base + lax.broadcasted_iota(jnp.int32, (span, 1), 0)
            qpos = start + lax.broadcasted_iota(jnp.int32, (1, tq), 1)
            bias = jnp.where(jnp.abs(qpos - kpos) <= WINDOW, 0.0, -jnp.inf)
            bias = jnp.concatenate([bias] * WC_G, axis=1)
        args = []
        for hh in range(nh):
            q4, k_cols, s_ctx, sink = head(hh)
            s_lat = None
            if latent:
                s_lat = lax.dot_general(k_ref[0, pl.ds(base, span), k_cols], q4, nt,
                                        preferred_element_type=F32) + bias
            args.append((hh, s_ctx, sink, s_lat, base if latent else None))
        in_range = functools.reduce(jnp.logical_and, [fast(*a) for a in args])

        @pl.when(jnp.logical_not(in_range))
        def _():
            for a in args:
                exact(*a)

    pl.when(i < n_lat_tiles)(functools.partial(tile, True))
    pl.when(i >= n_lat_tiles)(functools.partial(tile, False))


def _swa_attn(qkvz, sink_rows, *, seq, n_ctx, tq=SWA_TQ, nh=SWA_NH):
    b, t, _ = qkvz.shape
    gw = WC_G * WC_HD
    n_lat = seq // tq
    kern = functools.partial(_swa_kernel, tq=tq, nh=nh, seq=seq, n_ctx=n_ctx, n_lat_tiles=n_lat)
    z_col0, k_col0, v_col0 = WC_KV * gw, 2 * WC_KV * gw, 2 * WC_KV * gw + WC_KV * WC_HD
    assert k_col0 % (nh * WC_HD) == 0 and v_col0 % (nh * WC_HD) == 0 and z_col0 % (nh * gw) == 0
    k_blk0, v_blk0, z_blk0 = k_col0 // (nh * WC_HD), v_col0 // (nh * WC_HD), z_col0 // (nh * gw)
    return pl.pallas_call(
        kern,
        grid=(b, WC_KV // nh, t // tq),
        in_specs=[
            pl.BlockSpec((1, tq, nh * gw), lambda bi, n, i: (bi, i, n)),
            pl.BlockSpec((1, t, nh * WC_HD), lambda bi, n, i: (bi, 0, k_blk0 + n)),
            pl.BlockSpec((1, t, nh * WC_HD), lambda bi, n, i: (bi, 0, v_blk0 + n)),
            pl.BlockSpec((1, tq, nh * gw), lambda bi, n, i: (bi, i, z_blk0 + n)),
            pl.BlockSpec((nh, 1, WC_G * tq), lambda bi, n, i: (n, 0, 0)),
        ],
        out_specs=pl.BlockSpec((1, tq, nh * gw), lambda bi, n, i: (bi, i, n)),
        out_shape=jax.ShapeDtypeStruct((b, t, WC_KV * gw), BF16),
        scratch_shapes=[pltpu.VMEM((nh, WC_HD + ONES, t), BF16)],
        compiler_params=_params(
            ("parallel", "parallel", "arbitrary"),
            2 * nh * 2 * (3 * tq * gw + 2 * t * WC_HD)
            + nh * (WC_HD + ONES) * t * 2
            + 2 * nh * (tq + 2 * WINDOW + n_ctx) * WC_G * tq * 4),
        name="swa_attn",
    )(qkvz, qkvz, qkvz, qkvz, sink_rows)


HALO = 8


def _pool_kernel(x_ref, xp_ref, xn_ref, mod_ref, g_ref, w_ref, wg_ref, bg_ref, sc_ref,
                 o_ref, hb_ref, s2_ref, s4_ref, s8_ref, *, tm, seq, n_ctx, n_lat_tiles):
    i = pl.program_id(1)
    is_ctx = i >= n_lat_tiles
    shift = jnp.where(is_ctx, mod_ref[0, 3:4, :], mod_ref[0, 0:1, :])
    scale = jnp.where(is_ctx, mod_ref[0, 4:5, :], mod_ref[0, 1:2, :])
    g = g_ref[...]
    has_prev = jnp.logical_and(i != 0, i != n_lat_tiles)
    has_next = jnp.logical_and(i != n_lat_tiles - 1, i < n_lat_tiles)
    h = _norm_mod(x_ref[0], g, shift, scale)
    hb_ref[HALO:HALO + tm, :] = h
    hb_ref[0:HALO, :] = jnp.where(has_prev, _norm_mod(xp_ref[0], g, shift, scale), 0.0)
    hb_ref[HALO + tm:, :] = jnp.where(has_next, _norm_mod(xn_ref[0], g, shift, scale), 0.0)
    n = tm + 2 * HALO
    s2_ref[1:n, :] = hb_ref[0:n - 1, :] + hb_ref[1:n, :]
    s4_ref[2:n - 1, :] = s2_ref[1:n - 2, :] + s2_ref[3:n, :]
    s8_ref[4:n - 3, :] = s4_ref[2:n - 5, :] + s4_ref[6:n - 1, :]
    sums = (s2_ref[HALO:HALO + tm, :], s4_ref[HALO:HALO + tm, :], s8_ref[HALO:HALO + tm, :],
            s8_ref[HALO - 4:HALO - 4 + tm, :] + s8_ref[HALO + 4:HALO + 4 + tm, :])
    pos = jnp.where(is_ctx, 0, i * tm) + lax.broadcasted_iota(jnp.int32, (tm, 1), 0)
    seg_len = jnp.where(is_ctx, n_ctx, seq)
    hb16 = h.astype(BF16)
    d_inner = len(POOL_WINDOWS) * POOL_GW
    for gi, w in enumerate(POOL_WINDOWS):
        lo = jnp.maximum(pos - w // 2, 0)
        hi = jnp.minimum(pos - w // 2 + w, seg_len)
        cnt = (hi - lo).astype(F32)
        hd = (sums[gi] / cnt - h).astype(BF16)
        cols = slice(gi * POOL_GW, (gi + 1) * POOL_GW)
        d = jnp.dot(hd, w_ref[:, cols], preferred_element_type=F32)
        y = jnp.dot(d.astype(BF16), wg_ref[gi], preferred_element_type=F32) + bg_ref[:, cols]
        y = y * sc_ref[:, cols]
        z = jnp.dot(hb16, w_ref[:, d_inner + gi * POOL_GW:d_inner + (gi + 1) * POOL_GW],
                    preferred_element_type=F32)
        o_ref[0, :, cols] = (y * _silu(z)).astype(o_ref.dtype)


def _pool_mixer(xa, mod, g, w_in, w_grp, b_grp, scale, *, seq, n_ctx, tm=POOL_TM):
    b, t, d = xa.shape
    d_inner = len(POOL_WINDOWS) * POOL_GW
    hb = tm // HALO
    last = t // HALO - 1
    kern = functools.partial(_pool_kernel, tm=tm, seq=seq, n_ctx=n_ctx, n_lat_tiles=seq // tm)
    buf = pltpu.VMEM((tm + 2 * HALO, d), F32)
    return pl.pallas_call(
        kern,
        grid=(b, t // tm),
        in_specs=[
            pl.BlockSpec((1, tm, d), lambda bi, i: (bi, i, 0)),
            pl.BlockSpec((1, HALO, d), lambda bi, i: (bi, jnp.maximum(i * hb - 1, 0), 0)),
            pl.BlockSpec((1, HALO, d), lambda bi, i: (bi, jnp.minimum((i + 1) * hb, last), 0)),
            pl.BlockSpec((1, 8, d), lambda bi, i: (bi, 0, 0)),
            pl.BlockSpec((1, d), lambda bi, i: (0, 0)),
            pl.BlockSpec((d, 2 * d_inner), lambda bi, i: (0, 0)),
            pl.BlockSpec((len(POOL_WINDOWS), POOL_GW, POOL_GW), lambda bi, i: (0, 0, 0)),
            pl.BlockSpec((1, d_inner), lambda bi, i: (0, 0)),
            pl.BlockSpec((1, d_inner), lambda bi, i: (0, 0)),
        ],
        out_specs=pl.BlockSpec((1, tm, d_inner), lambda bi, i: (bi, i, 0)),
        out_shape=jax.ShapeDtypeStruct((b, t, d_inner), BF16),
        scratch_shapes=[buf, buf, buf, buf],
        compiler_params=_params(
            ("parallel", "arbitrary"),
            2 * (tm * d * 4 + 2 * d * d_inner * 2 + d_inner * POOL_GW * 2 + tm * d_inner * 2)
            + 4 * (tm + 2 * HALO) * d * 4
            + 6 * tm * d * 4),
        name="pool_mixer",
    )(xa, xa, xa, mod, g, w_in, w_grp, b_grp, scale)


def _out_kernel(g_ref, gc_ref, w_ref, x_ref, mod_ref, fg_ref, o_ref, *, tm, seq, final, split):
    i = pl.program_id(1)
    is_ctx = i * tm >= seq

    def emit(g, gate):
        y = jnp.dot(g, w_ref[...], preferred_element_type=F32)
        xn = x_ref[0] + gate * y
        if final:
            xn = xn * lax.rsqrt(jnp.mean(xn * xn, axis=-1, keepdims=True) + EPS) * fg_ref[...]
        o_ref[0] = xn

    if split:
        @pl.when(jnp.logical_not(is_ctx))
        def _():
            emit(g_ref[0], mod_ref[0, 2:3, :])

        @pl.when(is_ctx)
        def _():
            emit(gc_ref[0], mod_ref[0, 5:6, :])
    else:
        row = i * tm + lax.broadcasted_iota(jnp.int32, (tm, 1), 0)
        emit(g_ref[0], jnp.where(row >= seq, mod_ref[0, 5:6, :], mod_ref[0, 2:3, :]))


def _out_proj(gact, gctx, w, xa, mod, final_g, *, seq, final, tm):
    b, _, d_inner = gact.shape
    d = w.shape[1]
    rows = seq if final else xa.shape[1]
    split = gctx is not None
    if split:
        assert seq % tm == 0 and gctx.shape[1] == tm
        last_lat = seq // tm - 1
        g_map = lambda bi, i: (bi, jnp.minimum(i, last_lat), 0)
    else:
        gctx = gact
        g_map = lambda bi, i: (bi, i, 0)
    kern = functools.partial(_out_kernel, tm=tm, seq=seq, final=final, split=split)
    return pl.pallas_call(
        kern,
        grid=(b, rows // tm),
        in_specs=[
            pl.BlockSpec((1, tm, d_inner), g_map),
            pl.BlockSpec((1, tm, d_inner), lambda bi, i: (bi, 0, 0)),
            pl.BlockSpec((d_inner, d), lambda bi, i: (0, 0)),
            pl.BlockSpec((1, tm, d), lambda bi, i: (bi, i, 0)),
            pl.BlockSpec((1, 8, d), lambda bi, i: (bi, 0, 0)),
            pl.BlockSpec((1, d), lambda bi, i: (0, 0)),
        ],
        out_specs=pl.BlockSpec((1, tm, d), lambda bi, i: (bi, i, 0)),
        out_shape=jax.ShapeDtypeStruct((b, rows, d), F32),
        compiler_params=_params(
            ("parallel", "parallel"),
            2 * (2 * tm * d_inner * 2 + d_inner * d * 2 + 2 * tm * d * 4) + 2 * tm * d * 4),
        name="out_proj",
    )(gact, gctx, w, xa, mod, final_g)


def kernel(x, c, ctx, c_ctx, norm_g, w_ada, b_ada, a_w_in, a_w_out, a_lam_q1, a_lam_k1, a_lam_q2, a_lam_k2, a_subln_g, b_w_in, b_w_grp, b_b_grp, b_scale, b_w_out, c_w_in, c_sink, c_w_out, final_g):
    b, seq, d = x.shape
    n_ctx = ctx.shape[1]
    depth = w_ada.shape[0]
    xa = jnp.concatenate([x, ctx], axis=1)

    cond8 = jnp.concatenate([c, c_ctx[None, :], jnp.zeros((8 - b - 1, d), F32)], axis=0)
    ada = _adaln(cond8, w_ada, b_ada)

    cos_a, sin_a = _rope_tables(seq, n_ctx, DA_HD)
    cos_c, sin_c = _rope_tables(seq, n_ctx, WC_HD)

    out = None
    for i in range(depth):
        m = i % N_MIXERS
        j = i // N_MIXERS
        last = i == depth - 1
        lat = ada[i, :b].reshape(b, 3, d)
        cm = jnp.broadcast_to(ada[i, b].reshape(1, 3, d), (b, 3, d))
        mod = jnp.concatenate([lat, cm, jnp.zeros((b, 2, d), F32)], axis=1)
        g = norm_g[i].reshape(1, d)
        gctx = None

        if m == 0:
            lam_init = 0.8 - 0.6 * math.exp(-0.3 * i)
            n_qk = DA_HEADS * 2 * DA_HD
            w_in = _permute_rope_columns(a_w_in[j], 2 * n_qk, DA_HD).astype(BF16)
            sections = (("Q", n_qk), ("K", n_qk), ("P", 2 * DA_HEADS * DA_VD))
            qkvz = _proj(xa, mod, g, w_in, cos_a, sin_a, seq=seq, sections=sections,
                         qscale=DA_HD ** -0.5 * LOG2E)
            lamp = jnp.stack([a_lam_q1[j], a_lam_k1[j], a_lam_q2[j], a_lam_k2[j]], axis=0)
            sg = a_subln_g[j].reshape(1, DA_VD)
            gact = _diff_attn(qkvz, lamp, sg, seq=seq, n_ctx=n_ctx, lam_init=lam_init,
                              ctx_queries=False)
            if not last:
                gctx = _diff_attn(qkvz, lamp, sg, seq=seq, n_ctx=n_ctx, lam_init=lam_init,
                                  ctx_queries=True)
            w_out = a_w_out[j]
        elif m == 1:
            gact = _pool_mixer(xa, mod, g, b_w_in[j].astype(BF16), b_w_grp[j].astype(BF16),
                               b_b_grp[j].reshape(1, -1), b_scale[j].reshape(1, -1),
                               seq=seq, n_ctx=n_ctx)
            w_out = b_w_out[j]
        else:
            n_q, n_k = WC_KV * WC_G * WC_HD, WC_KV * WC_HD
            w_in = _permute_rope_columns(c_w_in[j], n_q + n_k, WC_HD)
            w_in = jnp.concatenate([w_in[:, :n_q], w_in[:, n_q + 2 * n_k:],
                                    w_in[:, n_q:n_q + 2 * n_k]], axis=1).astype(BF16)
            sections = (("Q", n_q), ("P", n_q), ("K", n_k), ("P", n_k))
            qkvz = _proj(xa, mod, g, w_in, cos_c, sin_c, seq=seq, sections=sections,
                         qscale=WC_HD ** -0.5 * LOG2E)
            sink_rows = jnp.repeat(c_sink[j].astype(F32) * LOG2E, SWA_TQ).reshape(
                WC_KV, 1, WC_G * SWA_TQ)
            gact = _swa_attn(qkvz, sink_rows, seq=seq, n_ctx=n_ctx)
            w_out = c_w_out[j]

        if last:
            tm = OUT_TM_FINAL
        else:
            tm = OUT_TM if gctx is None else n_ctx
        res = _out_proj(gact, gctx, w_out.astype(BF16), xa, mod, final_g.reshape(1, d), seq=seq,
                        final=last, tm=tm)
        if last:
            out = res
        else:
            xa = res
    return out
```

```python
import functools
import math

import jax
import jax.numpy as jnp
from jax import lax
from jax.experimental import pallas as pl
from jax.experimental.pallas import tpu as pltpu

F32 = jnp.float32
BF16 = jnp.bfloat16

GRID_W = 64
EPS = 1e-6
ROPE_BASE = 10000.0
N_MIXERS = 3
DA_HEADS = 16
DA_HD = 64
DA_VD = 128
POOL_WINDOWS = (2, 4, 8, 16)
POOL_GW = 512
WC_HD = 128
WC_KV = 4
WC_G = 4
WINDOW = 128

LOG2E = 1.4426950408889634
LANE = 128
SUB = 256
V7X_VMEM_BYTES = 64 * 1024 * 1024

PROJ_TM, PROJ_TN = 1088, 1024
OUT_TM = 544
OUT_TM_FINAL = 512
POOL_TM = 256
ATTN_TQ, ATTN_TK = 512, 512
SWA_TQ, SWA_NH = 256, 4


def _params(semantics, vmem_bytes):
    assert vmem_bytes <= V7X_VMEM_BYTES
    return pltpu.CompilerParams(dimension_semantics=semantics, vmem_limit_bytes=int(vmem_bytes))


def _silu(v):
    return v * jax.nn.sigmoid(v)


def _adaln_kernel(c_ref, w_ref, b_ref, o_ref):
    a = _silu(c_ref[...]).astype(BF16)
    o_ref[0] = jnp.dot(a, w_ref[0].astype(BF16), preferred_element_type=F32) + b_ref[0]


def _adaln(cond8, w_ada, b_ada):
    depth, d, d3 = w_ada.shape
    nt = d3 // d
    return pl.pallas_call(
        _adaln_kernel,
        grid=(depth, nt),
        in_specs=[
            pl.BlockSpec((8, d), lambda l, n: (0, 0)),
            pl.BlockSpec((1, d, d), lambda l, n: (l, 0, n)),
            pl.BlockSpec((1, 1, d), lambda l, n: (l, 0, n)),
        ],
        out_specs=pl.BlockSpec((1, 8, d), lambda l, n: (l, 0, n)),
        out_shape=jax.ShapeDtypeStruct((depth, 8, d3), F32),
        compiler_params=_params(("parallel", "parallel"), 2 * (d * d * 4 + 18 * d * 4) + d * d * 2),
        name="adaln",
    )(cond8, w_ada, b_ada.reshape(depth, 1, d3))


def _norm_mod(xf, g, shift, scale):
    inv = lax.rsqrt(jnp.mean(xf * xf, axis=-1, keepdims=True) + EPS)
    return xf * inv * (g * (1.0 + scale)) + shift


HALF = LANE // 2


def _rope_lane_layout(head_dim):
    quarter = head_dim // 4
    lane = jnp.arange(LANE)
    second = lane // HALF
    grp = (lane % HALF) // quarter
    freq = lane % quarter
    axis = grp % 2
    unit = grp // 2
    old = unit * head_dim + (axis * 2 + second) * quarter + freq
    sign = jnp.where(second == 0, -1.0, 1.0).astype(F32)
    return old, axis, freq, sign


def _rope_tables(seq, n_ctx, head_dim):
    quarter = head_dim // 4
    _, axis, freq, sign = _rope_lane_layout(head_dim)
    inv = ROPE_BASE ** (-jnp.arange(quarter, dtype=F32) / quarter)
    pos = jnp.arange(seq)
    coord = jnp.where(axis[None, :] == 0, (pos // GRID_W)[:, None], (pos % GRID_W)[:, None])
    ang = coord.astype(F32) * inv[freq][None, :]
    cos = jnp.concatenate([jnp.cos(ang), jnp.ones((n_ctx, LANE), F32)], axis=0)
    sin = jnp.concatenate([jnp.sin(ang) * sign[None, :], jnp.zeros((n_ctx, LANE), F32)], axis=0)
    return cos, sin


def _permute_rope_columns(w, n_cols, head_dim):
    quarter = head_dim // 4
    units = LANE // head_dim
    d = w.shape[0]
    head = w[:, :n_cols].reshape(d, n_cols // LANE, units, 2, 2, quarter)
    head = head.transpose(0, 1, 4, 2, 3, 5).reshape(d, n_cols)
    return jnp.concatenate([head, w[:, n_cols:]], axis=1)


def _proj_kernel(x_ref, mod_ref, g_ref, w_ref, cos_ref, sin_ref, o_ref, h_ref,
                 *, tm, seq, groups, qscale):
    i = pl.program_id(1)
    j = pl.program_id(2)

    def normalise(shift, scale):
        h_ref[...] = _norm_mod(x_ref[0], g_ref[...], shift, scale).astype(BF16)

    @pl.when(j == 0)
    def _():
        all_latent = (i + 1) * tm <= seq
        all_context = i * tm >= seq

        @pl.when(all_latent)
        def _():
            normalise(mod_ref[0, 0:1, :], mod_ref[0, 1:2, :])

        @pl.when(all_context)
        def _():
            normalise(mod_ref[0, 3:4, :], mod_ref[0, 4:5, :])

        @pl.when(jnp.logical_not(jnp.logical_or(all_latent, all_context)))
        def _():
            is_ctx = i * tm + lax.broadcasted_iota(jnp.int32, (tm, 1), 0) >= seq
            normalise(jnp.where(is_ctx, mod_ref[0, 3:4, :], mod_ref[0, 0:1, :]),
                      jnp.where(is_ctx, mod_ref[0, 4:5, :], mod_ref[0, 1:2, :]))

    def emit(kinds):
        tables = {}
        for kind in set(kinds) - {"P"}:
            mul = qscale if kind == "Q" else 1.0
            tables[kind] = (cos_ref[...] * mul, sin_ref[...] * mul)
        for s, kind in enumerate(kinds):
            acc = jnp.dot(h_ref[...], w_ref[:, s * SUB:(s + 1) * SUB],
                          preferred_element_type=F32)
            if kind == "P":
                o_ref[0, :, s * SUB:(s + 1) * SUB] = acc.astype(o_ref.dtype)
                continue
            cos, sin = tables[kind]
            for u in range(SUB // LANE):
                t = acc[:, u * LANE:(u + 1) * LANE]
                c0 = s * SUB + u * LANE
                o_ref[0, :, c0:c0 + LANE] = (t * cos + pltpu.roll(t, HALF, 1) * sin
                                             ).astype(o_ref.dtype)

    for j_lo, j_hi, kinds in groups:
        pl.when(jnp.logical_and(j >= j_lo, j <= j_hi))(functools.partial(emit, kinds))


def _proj(xa, mod, g, w, cos, sin, *, seq, sections, qscale, tm=PROJ_TM, tn=PROJ_TN):
    b, t, d = xa.shape
    n = w.shape[1]
    per_tile = tn // SUB
    kinds = [kind for kind, cols in sections for _ in range(cols // SUB)]
    assert len(kinds) * SUB == n
    tiles = [tuple(kinds[jt * per_tile:(jt + 1) * per_tile]) for jt in range(n // tn)]
    groups = []
    for jt, kt in enumerate(tiles):
        if groups and groups[-1][2] == kt:
            groups[-1] = (groups[-1][0], jt, kt)
        else:
            groups.append((jt, jt, kt))
    kern = functools.partial(_proj_kernel, tm=tm, seq=seq, groups=tuple(groups), qscale=qscale)
    return pl.pallas_call(
        kern,
        grid=(b, t // tm, n // tn),
        in_specs=[
            pl.BlockSpec((1, tm, d), lambda bi, i, j: (bi, i, 0)),
            pl.BlockSpec((1, 8, d), lambda bi, i, j: (bi, 0, 0)),
            pl.BlockSpec((1, d), lambda bi, i, j: (0, 0)),
            pl.BlockSpec((d, tn), lambda bi, i, j: (0, j)),
            pl.BlockSpec((tm, LANE), lambda bi, i, j: (i, 0)),
            pl.BlockSpec((tm, LANE), lambda bi, i, j: (i, 0)),
        ],
        out_specs=pl.BlockSpec((1, tm, tn), lambda bi, i, j: (bi, i, j)),
        out_shape=jax.ShapeDtypeStruct((b, t, n), BF16),
        scratch_shapes=[pltpu.VMEM((tm, d), BF16)],
        compiler_params=_params(("parallel", "parallel", "arbitrary"),
                                2 * (tm * d * 4 + d * tn * 2 + tm * tn * 2) + tm * d * 2
                                + 2 * tm * SUB * 4),
        name="proj",
    )(xa, mod, g, w, cos, sin)


PEAK_LIMIT = 2.0 ** 64
ONES = 16


def _stage_vt(v_ref, vt_ref):
    rows, vd = v_ref.shape[1], v_ref.shape[2]
    vt_ref[0:vd, :] = v_ref[0].T
    vt_ref[vd:vd + ONES, :] = jnp.ones((ONES, rows), vt_ref.dtype)


def _diff_attn_kernel(lamp_ref, q_ref, k_ref, v_ref, z_ref, sg_ref, o_ref, vt_ref, *acc_refs,
                      tiles, tk, lam_init):
    _stage_vt(v_ref, vt_ref)

    lp = lamp_ref[...]
    lam = (jnp.exp(jnp.sum(lp[0:1] * lp[1:2], axis=1, keepdims=True))
           - jnp.exp(jnp.sum(lp[2:3] * lp[3:4], axis=1, keepdims=True)) + lam_init)

    def queries(row0, rows):
        q = q_ref[0, row0:row0 + rows, :].astype(F32)
        lane = lax.broadcasted_iota(jnp.int32, q.shape, 1)
        map0 = lane % HALF < HALF // 2
        return jnp.concatenate([jnp.where(map0, q, 0.0),
                                jnp.where(map0, 0.0, q)], axis=0).astype(BF16)

    def scores(qm, c0, cl):
        return lax.dot_general(k_ref[0, pl.ds(c0, cl), :], qm, (((1,), (1,)), ((), ())),
                               preferred_element_type=F32)

    def pv(e, c0, cl):
        return jnp.dot(vt_ref[:, pl.ds(c0, cl)], e.astype(BF16),
                       preferred_element_type=F32)

    def finish(acc_ref, row0, rows):
        acc = acc_ref[...]
        inv = 1.0 / acc[DA_VD:DA_VD + 1, :]
        o_t = (acc[:DA_VD, :rows] * inv[:, :rows]
               - lam * (acc[:DA_VD, rows:] * inv[:, rows:]))
        o = o_t.T
        o = o * lax.rsqrt(jnp.mean(o * o, axis=-1, keepdims=True) + EPS) * sg_ref[...]
        o = o * (1.0 - lam_init)
        z = z_ref[0, row0:row0 + rows, :].astype(F32)
        o_ref[0, row0:row0 + rows, :] = (o * _silu(z)).astype(o_ref.dtype)

    def fast(acc_ref, qm, row0, rows, chunks):
        m = peak = None
        for c0, cl in chunks:
            s = scores(qm, c0, cl)
            if m is None:
                m = jnp.max(s[0:8], axis=0, keepdims=True)
            e = jnp.exp2(s - m).astype(BF16)
            emax = jnp.max(e, axis=0, keepdims=True).astype(F32)
            up = jnp.maximum(emax, 1.0)
            alpha = 1.0 / up
            if peak is None:
                acc_ref[...] = pv(e, c0, cl) * alpha
                peak = emax
            else:
                acc_ref[...] = (acc_ref[...] + pv(e, c0, cl)) * alpha
                peak = jnp.maximum(peak, emax)
            m = m + jnp.log2(up)
        finish(acc_ref, row0, rows)
        return jnp.max(peak)

    def exact(acc_ref, qm, row0, rows, chunks):
        s = scores(qm, *chunks[0])
        m = jnp.max(s, axis=0, keepdims=True)
        acc_ref[...] = pv(jnp.exp2(s - m), *chunks[0])

        def body(c, m):
            c0 = pl.multiple_of(chunks[1][0] + c * tk, tk)
            s = scores(qm, c0, tk)
            m_new = jnp.maximum(m, jnp.max(s, axis=0, keepdims=True))
            acc_ref[...] = (jnp.exp2(m - m_new) * acc_ref[...]
                            + pv(jnp.exp2(s - m_new), c0, tk))
            return m_new

        if len(chunks) > 1:
            lax.fori_loop(0, len(chunks) - 1, body, m)
        finish(acc_ref, row0, rows)

    work = [(acc_ref, queries(row0, rows), row0, rows, chunks)
            for acc_ref, (row0, rows, chunks) in zip(acc_refs, tiles)]
    peak = functools.reduce(jnp.maximum, [fast(*w) for w in work])

    @pl.when(jnp.logical_not(peak <= PEAK_LIMIT))
    def _():
        for w in work:
            exact(*w)


def _diff_attn(qkvz, lamp, sg, *, seq, n_ctx, lam_init, ctx_queries, tq=ATTN_TQ, tk=ATTN_TK):
    b, t, _ = qkvz.shape
    h = DA_HEADS
    all_keys = ((seq, n_ctx),) + tuple((c, tk) for c in range(0, seq, tk))
    tiles = tuple((r, tq, all_keys) for r in range(0, seq, tq))
    if ctx_queries:
        tiles += ((seq, n_ctx, ((seq, n_ctx),)),)
    q_rows = t if ctx_queries else seq
    kern = functools.partial(_diff_attn_kernel, tiles=tiles, tk=tk, lam_init=lam_init)
    acc_rows = sum(2 * rows for _, rows, _ in tiles)
    return pl.pallas_call(
        kern,
        grid=(b, h),
        in_specs=[
            pl.BlockSpec((4, DA_HD), lambda bi, hi: (0, 0)),
            pl.BlockSpec((1, q_rows, LANE), lambda bi, hi: (bi, 0, hi)),
            pl.BlockSpec((1, t, LANE), lambda bi, hi: (bi, 0, h + hi)),
            pl.BlockSpec((1, t, LANE), lambda bi, hi: (bi, 0, 2 * h + hi)),
            pl.BlockSpec((1, q_rows, LANE), lambda bi, hi: (bi, 0, 3 * h + hi)),
            pl.BlockSpec((1, DA_VD), lambda bi, hi: (0, 0)),
        ],
        out_specs=pl.BlockSpec((1, q_rows, LANE), lambda bi, hi: (bi, 0, hi)),
        out_shape=jax.ShapeDtypeStruct((b, q_rows, h * DA_VD), BF16),
        scratch_shapes=[pltpu.VMEM((DA_VD + ONES, t), BF16)]
        + [pltpu.VMEM((DA_VD + ONES, 2 * rows), F32) for _, rows, _ in tiles],
        compiler_params=_params(
            ("parallel", "parallel"),
            2 * LANE * 2 * (3 * q_rows + 2 * t)
            + (DA_VD + ONES) * (acc_rows * 4 + t * 2)
            + 4 * tk * 2 * tq * 4),
        name="diff_attn",
    )(lamp, qkvz, qkvz, qkvz, qkvz, sg)


def _swa_kernel(q_ref, k_ref, v_ref, z_ref, sink_ref, o_ref, vt_ref,
                *, tq, nh, seq, n_ctx, n_lat_tiles):
    i = pl.program_id(2)
    gw = WC_G * WC_HD

    @pl.when(i == 0)
    def _():
        for hh in range(nh):
            vt_ref[hh, 0:WC_HD, :] = v_ref[0, :, hh * WC_HD:(hh + 1) * WC_HD].T
            vt_ref[hh, WC_HD:WC_HD + ONES, :] = jnp.ones((ONES, v_ref.shape[1]), vt_ref.dtype)

    span = tq + 2 * WINDOW
    nt = (((1,), (1,)), ((), ()))

    def head(hh):
        q4 = jnp.concatenate([q_ref[0, :, hh * gw + g * WC_HD:hh * gw + (g + 1) * WC_HD]
                              for g in range(WC_G)], axis=0)
        k_cols = slice(hh * WC_HD, (hh + 1) * WC_HD)
        s_ctx = lax.dot_general(k_ref[0, seq:seq + n_ctx, k_cols], q4, nt,
                                preferred_element_type=F32)
        return q4, k_cols, s_ctx, sink_ref[hh]

    def finish(hh, o_t, den):
        o_t = o_t * (1.0 / den)
        for g in range(WC_G):
            cols = slice(hh * gw + g * WC_HD, hh * gw + (g + 1) * WC_HD)
            o = o_t[:, g * tq:(g + 1) * tq].T
            o_ref[0, :, cols] = (o * _silu(z_ref[0, :, cols].astype(F32))).astype(o_ref.dtype)

    def fast(hh, s_ctx, sink, s_lat, base):
        e_ctx = jnp.exp2(s_ctx).astype(BF16)
        acc = jnp.dot(vt_ref[hh, :, seq:seq + n_ctx], e_ctx, preferred_element_type=F32)
        peak = jnp.max(e_ctx, axis=0, keepdims=True).astype(F32)
        if s_lat is not None:
            e_lat = jnp.exp2(s_lat).astype(BF16)
            acc = acc + jnp.dot(vt_ref[hh, :, pl.ds(base, span)], e_lat,
                                preferred_element_type=F32)
            peak = jnp.maximum(peak, jnp.max(e_lat, axis=0, keepdims=True).astype(F32))
        e_sink = jnp.exp2(sink)
        den = acc[WC_HD:WC_HD + 1, :] + e_sink
        finish(hh, acc[:WC_HD, :], den)
        return jnp.logical_and(jnp.max(jnp.maximum(peak, e_sink)) <= PEAK_LIMIT,
                               jnp.min(den) >= 1.0 / PEAK_LIMIT)

    def exact(hh, s_ctx, sink, s_lat, base):
        m = jnp.maximum(jnp.max(s_ctx, axis=0, keepdims=True), sink)
        if s_lat is not None:
            m = jnp.maximum(m, jnp.max(s_lat, axis=0, keepdims=True))
        e_c = jnp.exp2(s_ctx - m)
        den = jnp.sum(e_c, axis=0, keepdims=True) + jnp.exp2(sink - m)
        o_t = jnp.dot(vt_ref[hh, :WC_HD, seq:seq + n_ctx], e_c.astype(BF16),
                      preferred_element_type=F32)
        if s_lat is not None:
            e_l = jnp.exp2(s_lat - m)
            den = den + jnp.sum(e_l, axis=0, keepdims=True)
            o_t = o_t + jnp.dot(vt_ref[hh, :WC_HD, pl.ds(base, span)], e_l.astype(BF16),
                                preferred_element_type=F32)
        finish(hh, o_t, den)

    def tile(latent):
        if latent:
            start = i * tq
            base = pl.multiple_of(jnp.clip(start - WINDOW, 0, seq - span), LANE)
            kpos = base + lax.broadcasted_iota(jnp.int32, (span, 1), 0)
            qpos = start + lax.broadcasted_iota(jnp.int32, (1, tq), 1)
            bias = jnp.where(jnp.abs(qpos - kpos) <= WINDOW, 0.0, -jnp.inf)
            bias = jnp.concatenate([bias] * WC_G, axis=1)
        args = []
        for hh in range(nh):
            q4, k_cols, s_ctx, sink = head(hh)
            s_lat = None
            if latent:
                s_lat = lax.dot_general(k_ref[0, pl.ds(base, span), k_cols], q4, nt,
                                        preferred_element_type=F32) + bias
            args.append((hh, s_ctx, sink, s_lat, base if latent else None))
        in_range = functools.reduce(jnp.logical_and, [fast(*a) for a in args])

        @pl.when(jnp.logical_not(in_range))
        def _():
            for a in args:
                exact(*a)

    pl.when(i < n_lat_tiles)(functools.partial(tile, True))
    pl.when(i >= n_lat_tiles)(functools.partial(tile, False))


def _swa_attn(qkvz, sink_rows, *, seq, n_ctx, tq=SWA_TQ, nh=SWA_NH):
    b, t, _ = qkvz.shape
    gw = WC_G * WC_HD
    n_lat = seq // tq
    kern = functools.partial(_swa_kernel, tq=tq, nh=nh, seq=seq, n_ctx=n_ctx, n_lat_tiles=n_lat)
    z_col0, k_col0, v_col0 = WC_KV * gw, 2 * WC_KV * gw, 2 * WC_KV * gw + WC_KV * WC_HD
    assert k_col0 % (nh * WC_HD) == 0 and v_col0 % (nh * WC_HD) == 0 and z_col0 % (nh * gw) == 0
    k_blk0, v_blk0, z_blk0 = k_col0 // (nh * WC_HD), v_col0 // (nh * WC_HD), z_col0 // (nh * gw)
    return pl.pallas_call(
        kern,
        grid=(b, WC_KV // nh, t // tq),
        in_specs=[
            pl.BlockSpec((1, tq, nh * gw), lambda bi, n, i: (bi, i, n)),
            pl.BlockSpec((1, t, nh * WC_HD), lambda bi, n, i: (bi, 0, k_blk0 + n)),
            pl.BlockSpec((1, t, nh * WC_HD), lambda bi, n, i: (bi, 0, v_blk0 + n)),
            pl.BlockSpec((1, tq, nh * gw), lambda bi, n, i: (bi, i, z_blk0 + n)),
            pl.BlockSpec((nh, 1, WC_G * tq), lambda bi, n, i: (n, 0, 0)),
        ],
        out_specs=pl.BlockSpec((1, tq, nh * gw), lambda bi, n, i: (bi, i, n)),
        out_shape=jax.ShapeDtypeStruct((b, t, WC_KV * gw), BF16),
        scratch_shapes=[pltpu.VMEM((nh, WC_HD + ONES, t), BF16)],
        compiler_params=_params(
            ("parallel", "parallel", "arbitrary"),
            2 * nh * 2 * (3 * tq * gw + 2 * t * WC_HD)
            + nh * (WC_HD + ONES) * t * 2
            + 2 * nh * (tq + 2 * WINDOW + n_ctx) * WC_G * tq * 4),
        name="swa_attn",
    )(qkvz, qkvz, qkvz, qkvz, sink_rows)


HALO = 8


def _pool_kernel(x_ref, xp_ref, xn_ref, mod_ref, g_ref, w_ref, wg_ref, bg_ref, sc_ref,
                 o_ref, hb_ref, s2_ref, s4_ref, s8_ref, *, tm, seq, n_ctx, n_lat_tiles):
    i = pl.program_id(1)
    is_ctx = i >= n_lat_tiles
    shift = jnp.where(is_ctx, mod_ref[0, 3:4, :], mod_ref[0, 0:1, :])
    scale = jnp.where(is_ctx, mod_ref[0, 4:5, :], mod_ref[0, 1:2, :])
    g = g_ref[...]
    has_prev = jnp.logical_and(i != 0, i != n_lat_tiles)
    has_next = jnp.logical_and(i != n_lat_tiles - 1, i < n_lat_tiles)
    h = _norm_mod(x_ref[0], g, shift, scale)
    hb_ref[HALO:HALO + tm, :] = h
    hb_ref[0:HALO, :] = jnp.where(has_prev, _norm_mod(xp_ref[0], g, shift, scale), 0.0)
    hb_ref[HALO + tm:, :] = jnp.where(has_next, _norm_mod(xn_ref[0], g, shift, scale), 0.0)
    n = tm + 2 * HALO
    s2_ref[1:n, :] = hb_ref[0:n - 1, :] + hb_ref[1:n, :]
    s4_ref[2:n - 1, :] = s2_ref[1:n - 2, :] + s2_ref[3:n, :]
    s8_ref[4:n - 3, :] = s4_ref[2:n - 5, :] + s4_ref[6:n - 1, :]
    sums = (s2_ref[HALO:HALO + tm, :], s4_ref[HALO:HALO + tm, :], s8_ref[HALO:HALO + tm, :],
            s8_ref[HALO - 4:HALO - 4 + tm, :] + s8_ref[HALO + 4:HALO + 4 + tm, :])
    pos = jnp.where(is_ctx, 0, i * tm) + lax.broadcasted_iota(jnp.int32, (tm, 1), 0)
    seg_len = jnp.where(is_ctx, n_ctx, seq)
    hb16 = h.astype(BF16)
    d_inner = len(POOL_WINDOWS) * POOL_GW
    for gi, w in enumerate(POOL_WINDOWS):
        lo = jnp.maximum(pos - w // 2, 0)
        hi = jnp.minimum(pos - w // 2 + w, seg_len)
        cnt = (hi - lo).astype(F32)
        hd = (sums[gi] / cnt - h).astype(BF16)
        cols = slice(gi * POOL_GW, (gi + 1) * POOL_GW)
        d = jnp.dot(hd, w_ref[:, cols], preferred_element_type=F32)
        y = jnp.dot(d.astype(BF16), wg_ref[gi], preferred_element_type=F32) + bg_ref[:, cols]
        y = y * sc_ref[:, cols]
        z = jnp.dot(hb16, w_ref[:, d_inner + gi * POOL_GW:d_inner + (gi + 1) * POOL_GW],
                    preferred_element_type=F32)
        o_ref[0, :, cols] = (y * _silu(z)).astype(o_ref.dtype)


def _pool_mixer(xa, mod, g, w_in, w_grp, b_grp, scale, *, seq, n_ctx, tm=POOL_TM):
    b, t, d = xa.shape
    d_inner = len(POOL_WINDOWS) * POOL_GW
    hb = tm // HALO
    last = t // HALO - 1
    kern = functools.partial(_pool_kernel, tm=tm, seq=seq, n_ctx=n_ctx, n_lat_tiles=seq // tm)
    buf = pltpu.VMEM((tm + 2 * HALO, d), F32)
    return pl.pallas_call(
        kern,
        grid=(b, t // tm),
        in_specs=[
            pl.BlockSpec((1, tm, d), lambda bi, i: (bi, i, 0)),
            pl.BlockSpec((1, HALO, d), lambda bi, i: (bi, jnp.maximum(i * hb - 1, 0), 0)),
            pl.BlockSpec((1, HALO, d), lambda bi, i: (bi, jnp.minimum((i + 1) * hb, last), 0)),
            pl.BlockSpec((1, 8, d), lambda bi, i: (bi, 0, 0)),
            pl.BlockSpec((1, d), lambda bi, i: (0, 0)),
            pl.BlockSpec((d, 2 * d_inner), lambda bi, i: (0, 0)),
            pl.BlockSpec((len(POOL_WINDOWS), POOL_GW, POOL_GW), lambda bi, i: (0, 0, 0)),
            pl.BlockSpec((1, d_inner), lambda bi, i: (0, 0)),
            pl.BlockSpec((1, d_inner), lambda bi, i: (0, 0)),
        ],
        out_specs=pl.BlockSpec((1, tm, d_inner), lambda bi, i: (bi, i, 0)),
        out_shape=jax.ShapeDtypeStruct((b, t, d_inner), BF16),
        scratch_shapes=[buf, buf, buf, buf],
        compiler_params=_params(
            ("parallel", "arbitrary"),
            2 * (tm * d * 4 + 2 * d * d_inner * 2 + d_inner * POOL_GW * 2 + tm * d_inner * 2)
            + 4 * (tm + 2 * HALO) * d * 4
            + 6 * tm * d * 4),
        name="pool_mixer",
    )(xa, xa, xa, mod, g, w_in, w_grp, b_grp, scale)


def _out_kernel(g_ref, w_ref, x_ref, mod_ref, fg_ref, o_ref, *, tm, seq, final):
    i = pl.program_id(1)
    y = jnp.dot(g_ref[0], w_ref[...], preferred_element_type=F32)
    row = i * tm + lax.broadcasted_iota(jnp.int32, (tm, 1), 0)
    gate = jnp.where(row >= seq, mod_ref[0, 5:6, :], mod_ref[0, 2:3, :])
    xn = x_ref[0] + gate * y
    if final:
        xn = xn * lax.rsqrt(jnp.mean(xn * xn, axis=-1, keepdims=True) + EPS) * fg_ref[...]
    o_ref[0] = xn


def _out_proj(gact, w, xa, mod, final_g, *, seq, final, tm):
    b, _, d_inner = gact.shape
    d = w.shape[1]
    rows = seq if final else xa.shape[1]
    kern = functools.partial(_out_kernel, tm=tm, seq=seq, final=final)
    return pl.pallas_call(
        kern,
        grid=(b, rows // tm),
        in_specs=[
            pl.BlockSpec((1, tm, d_inner), lambda bi, i: (bi, i, 0)),
            pl.BlockSpec((d_inner, d), lambda bi, i: (0, 0)),
            pl.BlockSpec((1, tm, d), lambda bi, i: (bi, i, 0)),
            pl.BlockSpec((1, 8, d), lambda bi, i: (bi, 0, 0)),
            pl.BlockSpec((1, d), lambda bi, i: (0, 0)),
        ],
        out_specs=pl.BlockSpec((1, tm, d), lambda bi, i: (bi, i, 0)),
        out_shape=jax.ShapeDtypeStruct((b, rows, d), F32),
        compiler_params=_params(
            ("parallel", "parallel"),
            2 * (tm * d_inner * 2 + d_inner * d * 2 + 2 * tm * d * 4) + 2 * tm * d * 4),
        name="out_proj",
    )(gact, w, xa, mod, final_g)


def kernel(x, c, ctx, c_ctx, norm_g, w_ada, b_ada, a_w_in, a_w_out, a_lam_q1, a_lam_k1, a_lam_q2, a_lam_k2, a_subln_g, b_w_in, b_w_grp, b_b_grp, b_scale, b_w_out, c_w_in, c_sink, c_w_out, final_g):
    b, seq, d = x.shape
    n_ctx = ctx.shape[1]
    depth = w_ada.shape[0]
    xa = jnp.concatenate([x, ctx], axis=1)

    cond8 = jnp.concatenate([c, c_ctx[None, :], jnp.zeros((8 - b - 1, d), F32)], axis=0)
    ada = _adaln(cond8, w_ada, b_ada)

    cos_a, sin_a = _rope_tables(seq, n_ctx, DA_HD)
    cos_c, sin_c = _rope_tables(seq, n_ctx, WC_HD)

    out = None
    for i in range(depth):
        m = i % N_MIXERS
        j = i // N_MIXERS
        last = i == depth - 1
        lat = ada[i, :b].reshape(b, 3, d)
        cm = jnp.broadcast_to(ada[i, b].reshape(1, 3, d), (b, 3, d))
        mod = jnp.concatenate([lat, cm, jnp.zeros((b, 2, d), F32)], axis=1)
        g = norm_g[i].reshape(1, d)

        if m == 0:
            lam_init = 0.8 - 0.6 * math.exp(-0.3 * i)
            n_qk = DA_HEADS * 2 * DA_HD
            w_in = _permute_rope_columns(a_w_in[j], 2 * n_qk, DA_HD).astype(BF16)
            sections = (("Q", n_qk), ("K", n_qk), ("P", 2 * DA_HEADS * DA_VD))
            qkvz = _proj(xa, mod, g, w_in, cos_a, sin_a, seq=seq, sections=sections,
                         qscale=DA_HD ** -0.5 * LOG2E)
            lamp = jnp.stack([a_lam_q1[j], a_lam_k1[j], a_lam_q2[j], a_lam_k2[j]], axis=0)
            sg = a_subln_g[j].reshape(1, DA_VD)
            gact = _diff_attn(qkvz, lamp, sg, seq=seq, n_ctx=n_ctx, lam_init=lam_init,
                              ctx_queries=not last)
            w_out = a_w_out[j]
        elif m == 1:
            gact = _pool_mixer(xa, mod, g, b_w_in[j].astype(BF16), b_w_grp[j].astype(BF16),
                               b_b_grp[j].reshape(1, -1), b_scale[j].reshape(1, -1),
                               seq=seq, n_ctx=n_ctx)
            w_out = b_w_out[j]
        else:
            n_q, n_k = WC_KV * WC_G * WC_HD, WC_KV * WC_HD
            w_in = _permute_rope_columns(c_w_in[j], n_q + n_k, WC_HD)
            w_in = jnp.concatenate([w_in[:, :n_q], w_in[:, n_q + 2 * n_k:],
                                    w_in[:, n_q:n_q + 2 * n_k]], axis=1).astype(BF16)
            sections = (("Q", n_q), ("P", n_q), ("K", n_k), ("P", n_k))
            qkvz = _proj(xa, mod, g, w_in, cos_c, sin_c, seq=seq, sections=sections,
                         qscale=WC_HD ** -0.5 * LOG2E)
            sink_rows = jnp.repeat(c_sink[j].astype(F32) * LOG2E, SWA_TQ).reshape(
                WC_KV, 1, WC_G * SWA_TQ)
            gact = _swa_attn(qkvz, sink_rows, seq=seq, n_ctx=n_ctx)
            w_out = c_w_out[j]

        res = _out_proj(gact, w_out.astype(BF16), xa, mod, final_g.reshape(1, d), seq=seq,
                        final=last, tm=OUT_TM_FINAL if last else OUT_TM)
        if last:
            out = res
        else:
            xa = res
    return out
```

```python
import functools
import math

import jax
import jax.numpy as jnp
from jax import lax
from jax.experimental import pallas as pl
from jax.experimental.pallas import tpu as pltpu

F32 = jnp.float32
BF16 = jnp.bfloat16

GRID_W = 64
EPS = 1e-6
ROPE_BASE = 10000.0
N_MIXERS = 3
DA_HEADS = 16
DA_HD = 64
DA_VD = 128
POOL_WINDOWS = (2, 4, 8, 16)
POOL_GW = 512
WC_HD = 128
WC_KV = 4
WC_G = 4
WINDOW = 128

LOG2E = 1.4426950408889634
LANE = 128
SUB = 256
V7X_VMEM_BYTES = 64 * 1024 * 1024

PROJ_TM, PROJ_TN = 1088, 1024
OUT_TM = 544
OUT_TM_FINAL = 512
POOL_TM = 256
ATTN_TQ, ATTN_TK = 512, 512
ATTN_NSUB = 4
SWA_TQ, SWA_NH = 256, 4


def _params(semantics, vmem_bytes):
    assert vmem_bytes <= V7X_VMEM_BYTES
    return pltpu.CompilerParams(dimension_semantics=semantics, vmem_limit_bytes=int(vmem_bytes))


def _silu(v):
    return v * jax.nn.sigmoid(v)


def _adaln_kernel(c_ref, w_ref, b_ref, o_ref):
    a = _silu(c_ref[...]).astype(BF16)
    o_ref[0] = jnp.dot(a, w_ref[0].astype(BF16), preferred_element_type=F32) + b_ref[0]


def _adaln(cond8, w_ada, b_ada):
    depth, d, d3 = w_ada.shape
    nt = d3 // d
    return pl.pallas_call(
        _adaln_kernel,
        grid=(depth, nt),
        in_specs=[
            pl.BlockSpec((8, d), lambda l, n: (0, 0)),
            pl.BlockSpec((1, d, d), lambda l, n: (l, 0, n)),
            pl.BlockSpec((1, 1, d), lambda l, n: (l, 0, n)),
        ],
        out_specs=pl.BlockSpec((1, 8, d), lambda l, n: (l, 0, n)),
        out_shape=jax.ShapeDtypeStruct((depth, 8, d3), F32),
        compiler_params=_params(("parallel", "parallel"), 2 * (d * d * 4 + 18 * d * 4) + d * d * 2),
        name="adaln",
    )(cond8, w_ada, b_ada.reshape(depth, 1, d3))


def _norm_mod(xf, g, shift, scale):
    inv = lax.rsqrt(jnp.mean(xf * xf, axis=-1, keepdims=True) + EPS)
    return xf * inv * (g * (1.0 + scale)) + shift


HALF = LANE // 2


def _rope_lane_layout(head_dim):
    quarter = head_dim // 4
    lane = jnp.arange(LANE)
    second = lane // HALF
    grp = (lane % HALF) // quarter
    freq = lane % quarter
    axis = grp % 2
    unit = grp // 2
    old = unit * head_dim + (axis * 2 + second) * quarter + freq
    sign = jnp.where(second == 0, -1.0, 1.0).astype(F32)
    return old, axis, freq, sign


def _rope_tables(seq, n_ctx, head_dim):
    quarter = head_dim // 4
    _, axis, freq, sign = _rope_lane_layout(head_dim)
    inv = ROPE_BASE ** (-jnp.arange(quarter, dtype=F32) / quarter)
    pos = jnp.arange(seq)
    coord = jnp.where(axis[None, :] == 0, (pos // GRID_W)[:, None], (pos % GRID_W)[:, None])
    ang = coord.astype(F32) * inv[freq][None, :]
    cos = jnp.concatenate([jnp.cos(ang), jnp.ones((n_ctx, LANE), F32)], axis=0)
    sin = jnp.concatenate([jnp.sin(ang) * sign[None, :], jnp.zeros((n_ctx, LANE), F32)], axis=0)
    return cos, sin


def _permute_rope_columns(w, n_cols, head_dim):
    quarter = head_dim // 4
    units = LANE // head_dim
    d = w.shape[0]
    head = w[:, :n_cols].reshape(d, n_cols // LANE, units, 2, 2, quarter)
    head = head.transpose(0, 1, 4, 2, 3, 5).reshape(d, n_cols)
    return jnp.concatenate([head, w[:, n_cols:]], axis=1)


def _proj_kernel(x_ref, mod_ref, g_ref, w_ref, cos_ref, sin_ref, o_ref, h_ref,
                 *, tm, seq, groups, qscale):
    i = pl.program_id(1)
    j = pl.program_id(2)

    def normalise(shift, scale):
        h_ref[...] = _norm_mod(x_ref[0], g_ref[...], shift, scale).astype(BF16)

    @pl.when(j == 0)
    def _():
        all_latent = (i + 1) * tm <= seq
        all_context = i * tm >= seq

        @pl.when(all_latent)
        def _():
            normalise(mod_ref[0, 0:1, :], mod_ref[0, 1:2, :])

        @pl.when(all_context)
        def _():
            normalise(mod_ref[0, 3:4, :], mod_ref[0, 4:5, :])

        @pl.when(jnp.logical_not(jnp.logical_or(all_latent, all_context)))
        def _():
            is_ctx = i * tm + lax.broadcasted_iota(jnp.int32, (tm, 1), 0) >= seq
            normalise(jnp.where(is_ctx, mod_ref[0, 3:4, :], mod_ref[0, 0:1, :]),
                      jnp.where(is_ctx, mod_ref[0, 4:5, :], mod_ref[0, 1:2, :]))

    def emit(kinds):
        tables = {}
        for kind in set(kinds) - {"P"}:
            mul = qscale if kind == "Q" else 1.0
            tables[kind] = (cos_ref[...] * mul, sin_ref[...] * mul)
        for s, kind in enumerate(kinds):
            acc = jnp.dot(h_ref[...], w_ref[:, s * SUB:(s + 1) * SUB],
                          preferred_element_type=F32)
            if kind == "P":
                o_ref[0, :, s * SUB:(s + 1) * SUB] = acc.astype(o_ref.dtype)
                continue
            cos, sin = tables[kind]
            for u in range(SUB // LANE):
                t = acc[:, u * LANE:(u + 1) * LANE]
                c0 = s * SUB + u * LANE
                o_ref[0, :, c0:c0 + LANE] = (t * cos + pltpu.roll(t, HALF, 1) * sin
                                             ).astype(o_ref.dtype)

    for j_lo, j_hi, kinds in groups:
        pl.when(jnp.logical_and(j >= j_lo, j <= j_hi))(functools.partial(emit, kinds))


def _proj(xa, mod, g, w, cos, sin, *, seq, sections, qscale, tm=PROJ_TM, tn=PROJ_TN):
    b, t, d = xa.shape
    n = w.shape[1]
    per_tile = tn // SUB
    kinds = [kind for kind, cols in sections for _ in range(cols // SUB)]
    assert len(kinds) * SUB == n
    tiles = [tuple(kinds[jt * per_tile:(jt + 1) * per_tile]) for jt in range(n // tn)]
    groups = []
    for jt, kt in enumerate(tiles):
        if groups and groups[-1][2] == kt:
            groups[-1] = (groups[-1][0], jt, kt)
        else:
            groups.append((jt, jt, kt))
    kern = functools.partial(_proj_kernel, tm=tm, seq=seq, groups=tuple(groups), qscale=qscale)
    return pl.pallas_call(
        kern,
        grid=(b, t // tm, n // tn),
        in_specs=[
            pl.BlockSpec((1, tm, d), lambda bi, i, j: (bi, i, 0)),
            pl.BlockSpec((1, 8, d), lambda bi, i, j: (bi, 0, 0)),
            pl.BlockSpec((1, d), lambda bi, i, j: (0, 0)),
            pl.BlockSpec((d, tn), lambda bi, i, j: (0, j)),
            pl.BlockSpec((tm, LANE), lambda bi, i, j: (i, 0)),
            pl.BlockSpec((tm, LANE), lambda bi, i, j: (i, 0)),
        ],
        out_specs=pl.BlockSpec((1, tm, tn), lambda bi, i, j: (bi, i, j)),
        out_shape=jax.ShapeDtypeStruct((b, t, n), BF16),
        scratch_shapes=[pltpu.VMEM((tm, d), BF16)],
        compiler_params=_params(("parallel", "parallel", "arbitrary"),
                                2 * (tm * d * 4 + d * tn * 2 + tm * tn * 2) + tm * d * 2
                                + 2 * tm * SUB * 4),
        name="proj",
    )(xa, mod, g, w, cos, sin)


PEAK_LIMIT = 2.0 ** 64
ONES = 16


def _stage_vt(v_ref, vt_ref):
    rows, vd = v_ref.shape[1], v_ref.shape[2]
    vt_ref[0:vd, :] = v_ref[0].T
    vt_ref[vd:vd + ONES, :] = jnp.ones((ONES, rows), vt_ref.dtype)


def _diff_attn_kernel(lamp_ref, q_ref, k_ref, v_ref, z_ref, sg_ref, o_ref, vt_ref, *acc_refs,
                      main_tiles, ctx_tiles, n_main, tk, lam_init):
    i = pl.program_id(2)

    @pl.when(i == 0)
    def _():
        _stage_vt(v_ref, vt_ref)

    lp = lamp_ref[...]
    lam = (jnp.exp(jnp.sum(lp[0:1] * lp[1:2], axis=1, keepdims=True))
           - jnp.exp(jnp.sum(lp[2:3] * lp[3:4], axis=1, keepdims=True)) + lam_init)

    def queries(row0, rows):
        q = q_ref[0, row0:row0 + rows, :].astype(F32)
        lane = lax.broadcasted_iota(jnp.int32, q.shape, 1)
        map0 = lane % HALF < HALF // 2
        return jnp.concatenate([jnp.where(map0, q, 0.0),
                                jnp.where(map0, 0.0, q)], axis=0).astype(BF16)

    def scores(qm, c0, cl):
        return lax.dot_general(k_ref[0, pl.ds(c0, cl), :], qm, (((1,), (1,)), ((), ())),
                               preferred_element_type=F32)

    def pv(e, c0, cl):
        return jnp.dot(vt_ref[:, pl.ds(c0, cl)], e.astype(BF16),
                       preferred_element_type=F32)

    def finish(acc_ref, row0, rows):
        acc = acc_ref[...]
        inv = 1.0 / acc[DA_VD:DA_VD + 1, :]
        o_t = (acc[:DA_VD, :rows] * inv[:, :rows]
               - lam * (acc[:DA_VD, rows:] * inv[:, rows:]))
        o = o_t.T
        o = o * lax.rsqrt(jnp.mean(o * o, axis=-1, keepdims=True) + EPS) * sg_ref[...]
        o = o * (1.0 - lam_init)
        z = z_ref[0, row0:row0 + rows, :].astype(F32)
        o_ref[0, row0:row0 + rows, :] = (o * _silu(z)).astype(o_ref.dtype)

    def fast(acc_ref, qm, row0, rows, chunks):
        m = peak = None
        for c0, cl in chunks:
            s = scores(qm, c0, cl)
            if m is None:
                m = jnp.max(s[0:8], axis=0, keepdims=True)
            e = jnp.exp2(s - m).astype(BF16)
            emax = jnp.max(e, axis=0, keepdims=True).astype(F32)
            up = jnp.maximum(emax, 1.0)
            alpha = 1.0 / up
            if peak is None:
                acc_ref[...] = pv(e, c0, cl) * alpha
                peak = emax
            else:
                acc_ref[...] = (acc_ref[...] + pv(e, c0, cl)) * alpha
                peak = jnp.maximum(peak, emax)
            m = m + jnp.log2(up)
        finish(acc_ref, row0, rows)
        return jnp.max(peak)

    def exact(acc_ref, qm, row0, rows, chunks):
        s = scores(qm, *chunks[0])
        m = jnp.max(s, axis=0, keepdims=True)
        acc_ref[...] = pv(jnp.exp2(s - m), *chunks[0])

        def body(c, m):
            c0 = pl.multiple_of(chunks[1][0] + c * tk, tk)
            s = scores(qm, c0, tk)
            m_new = jnp.maximum(m, jnp.max(s, axis=0, keepdims=True))
            acc_ref[...] = (jnp.exp2(m - m_new) * acc_ref[...]
                            + pv(jnp.exp2(s - m_new), c0, tk))
            return m_new

        if len(chunks) > 1:
            lax.fori_loop(0, len(chunks) - 1, body, m)
        finish(acc_ref, row0, rows)

    def run(tiles, accs):
        work = [(acc_ref, queries(row0, rows), row0, rows, chunks)
                for acc_ref, (row0, rows, chunks) in zip(accs, tiles)]
        peak = functools.reduce(jnp.maximum, [fast(*w) for w in work])

        @pl.when(jnp.logical_not(peak <= PEAK_LIMIT))
        def _():
            for w in work:
                exact(*w)

    pl.when(i < n_main)(functools.partial(run, main_tiles, acc_refs[:len(main_tiles)]))
    if ctx_tiles:
        pl.when(i == n_main)(functools.partial(run, ctx_tiles, acc_refs[len(main_tiles):]))


def _diff_attn(qkvz, lamp, sg, *, seq, n_ctx, lam_init, ctx_queries,
               tq=ATTN_TQ, nsub=ATTN_NSUB, tk=ATTN_TK):
    b, t, _ = qkvz.shape
    h = DA_HEADS
    tqs = tq * nsub
    n_main = seq // tqs
    assert n_main * tqs == seq and n_ctx <= tqs
    all_keys = ((seq, n_ctx),) + tuple((c, tk) for c in range(0, seq, tk))
    main_tiles = tuple((u * tq, tq, all_keys) for u in range(nsub))
    ctx_tiles = ((0, n_ctx, ((seq, n_ctx),)),) if ctx_queries else ()
    out_rows = t if ctx_queries else seq
    kern = functools.partial(_diff_attn_kernel, main_tiles=main_tiles, ctx_tiles=ctx_tiles,
                             n_main=n_main, tk=tk, lam_init=lam_init)
    acc_rows = [2 * rows for _, rows, _ in main_tiles + ctx_tiles]
    return pl.pallas_call(
        kern,
        grid=(b, h, n_main + len(ctx_tiles)),
        in_specs=[
            pl.BlockSpec((4, DA_HD), lambda bi, hi, i: (0, 0)),
            pl.BlockSpec((1, tqs, LANE), lambda bi, hi, i: (bi, i, hi)),
            pl.BlockSpec((1, t, LANE), lambda bi, hi, i: (bi, 0, h + hi)),
            pl.BlockSpec((1, t, LANE), lambda bi, hi, i: (bi, 0, 2 * h + hi)),
            pl.BlockSpec((1, tqs, LANE), lambda bi, hi, i: (bi, i, 3 * h + hi)),
            pl.BlockSpec((1, DA_VD), lambda bi, hi, i: (0, 0)),
        ],
        out_specs=pl.BlockSpec((1, tqs, LANE), lambda bi, hi, i: (bi, i, hi)),
        out_shape=jax.ShapeDtypeStruct((b, out_rows, h * DA_VD), BF16),
        scratch_shapes=[pltpu.VMEM((DA_VD + ONES, t), BF16)]
        + [pltpu.VMEM((DA_VD + ONES, r), F32) for r in acc_rows],
        compiler_params=_params(
            ("parallel", "parallel", "arbitrary"),
            2 * LANE * 2 * (3 * tqs + 2 * t)
            + (DA_VD + ONES) * (sum(acc_rows) * 4 + t * 2)
            + 4 * tk * 2 * tq * 4),
        name="diff_attn",
    )(lamp, qkvz, qkvz, qkvz, qkvz, sg)


def _swa_kernel(q_ref, k_ref, v_ref, z_ref, sink_ref, o_ref, vt_ref,
                *, tq, nh, seq, n_ctx, n_lat_tiles):
    i = pl.program_id(2)
    gw = WC_G * WC_HD

    @pl.when(i == 0)
    def _():
        for hh in range(nh):
            vt_ref[hh, 0:WC_HD, :] = v_ref[0, :, hh * WC_HD:(hh + 1) * WC_HD].T
            vt_ref[hh, WC_HD:WC_HD + ONES, :] = jnp.ones((ONES, v_ref.shape[1]), vt_ref.dtype)

    span = tq + 2 * WINDOW
    nt = (((1,), (1,)), ((), ()))

    def head(hh):
        q4 = jnp.concatenate([q_ref[0, :, hh * gw + g * WC_HD:hh * gw + (g + 1) * WC_HD]
                              for g in range(WC_G)], axis=0)
        k_cols = slice(hh * WC_HD, (hh + 1) * WC_HD)
        s_ctx = lax.dot_general(k_ref[0, seq:seq + n_ctx, k_cols], q4, nt,
                                preferred_element_type=F32)
        return q4, k_cols, s_ctx, sink_ref[hh]

    def finish(hh, o_t, den):
        o_t = o_t * (1.0 / den)
        for g in range(WC_G):
            cols = slice(hh * gw + g * WC_HD, hh * gw + (g + 1) * WC_HD)
            o = o_t[:, g * tq:(g + 1) * tq].T
            o_ref[0, :, cols] = (o * _silu(z_ref[0, :, cols].astype(F32))).astype(o_ref.dtype)

    def fast(hh, s_ctx, sink, s_lat, base):
        e_ctx = jnp.exp2(s_ctx).astype(BF16)
        acc = jnp.dot(vt_ref[hh, :, seq:seq + n_ctx], e_ctx, preferred_element_type=F32)
        peak = jnp.max(e_ctx, axis=0, keepdims=True).astype(F32)
        if s_lat is not None:
            e_lat = jnp.exp2(s_lat).astype(BF16)
            acc = acc + jnp.dot(vt_ref[hh, :, pl.ds(base, span)], e_lat,
                                preferred_element_type=F32)
            peak = jnp.maximum(peak, jnp.max(e_lat, axis=0, keepdims=True).astype(F32))
        e_sink = jnp.exp2(sink)
        den = acc[WC_HD:WC_HD + 1, :] + e_sink
        finish(hh, acc[:WC_HD, :], den)
        return jnp.logical_and(jnp.max(jnp.maximum(peak, e_sink)) <= PEAK_LIMIT,
                               jnp.min(den) >= 1.0 / PEAK_LIMIT)

    def exact(hh, s_ctx, sink, s_lat, base):
        m = jnp.maximum(jnp.max(s_ctx, axis=0, keepdims=True), sink)
        if s_lat is not None:
            m = jnp.maximum(m, jnp.max(s_lat, axis=0, keepdims=True))
        e_c = jnp.exp2(s_ctx - m)
        den = jnp.sum(e_c, axis=0, keepdims=True) + jnp.exp2(sink - m)
        o_t = jnp.dot(vt_ref[hh, :WC_HD, seq:seq + n_ctx], e_c.astype(BF16),
                      preferred_element_type=F32)
        if s_lat is not None:
            e_l = jnp.exp2(s_lat - m)
            den = den + jnp.sum(e_l, axis=0, keepdims=True)
            o_t = o_t + jnp.dot(vt_ref[hh, :WC_HD, pl.ds(base, span)], e_l.astype(BF16),
                                preferred_element_type=F32)
        finish(hh, o_t, den)

    def tile(latent):
        if latent:
            start = i * tq
            base = pl.multiple_of(jnp.clip(start - WINDOW, 0, seq - span), LANE)
            kpos = base + lax.broadcasted_iota(jnp.int32, (span, 1), 0)
            qpos = start + lax.broadcasted_iota(jnp.int32, (1, tq), 1)
            bias = jnp.where(jnp.abs(qpos - kpos) <= WINDOW, 0.0, -jnp.inf)
            bias = jnp.concatenate([bias] * WC_G, axis=1)
        args = []
        for hh in range(nh):
            q4, k_cols, s_ctx, sink = head(hh)
            s_lat = None
            if latent:
                s_lat = lax.dot_general(k_ref[0, pl.ds(base, span), k_cols], q4, nt,
                                        preferred_element_type=F32) + bias
            args.append((hh, s_ctx, sink, s_lat, base if latent else None))
        in_range = functools.reduce(jnp.logical_and, [fast(*a) for a in args])

        @pl.when(jnp.logical_not(in_range))
        def _():
            for a in args:
                exact(*a)

    pl.when(i < n_lat_tiles)(functools.partial(tile, True))
    pl.when(i >= n_lat_tiles)(functools.partial(tile, False))


def _swa_attn(qkvz, sink_rows, *, seq, n_ctx, tq=SWA_TQ, nh=SWA_NH):
    b, t, _ = qkvz.shape
    gw = WC_G * WC_HD
    n_lat = seq // tq
    kern = functools.partial(_swa_kernel, tq=tq, nh=nh, seq=seq, n_ctx=n_ctx, n_lat_tiles=n_lat)
    z_col0, k_col0, v_col0 = WC_KV * gw, 2 * WC_KV * gw, 2 * WC_KV * gw + WC_KV * WC_HD
    assert k_col0 % (nh * WC_HD) == 0 and v_col0 % (nh * WC_HD) == 0 and z_col0 % (nh * gw) == 0
    k_blk0, v_blk0, z_blk0 = k_col0 // (nh * WC_HD), v_col0 // (nh * WC_HD), z_col0 // (nh * gw)
    return pl.pallas_call(
        kern,
        grid=(b, WC_KV // nh, t // tq),
        in_specs=[
            pl.BlockSpec((1, tq, nh * gw), lambda bi, n, i: (bi, i, n)),
            pl.BlockSpec((1, t, nh * WC_HD), lambda bi, n, i: (bi, 0, k_blk0 + n)),
            pl.BlockSpec((1, t, nh * WC_HD), lambda bi, n, i: (bi, 0, v_blk0 + n)),
            pl.BlockSpec((1, tq, nh * gw), lambda bi, n, i: (bi, i, z_blk0 + n)),
            pl.BlockSpec((nh, 1, WC_G * tq), lambda bi, n, i: (n, 0, 0)),
        ],
        out_specs=pl.BlockSpec((1, tq, nh * gw), lambda bi, n, i: (bi, i, n)),
        out_shape=jax.ShapeDtypeStruct((b, t, WC_KV * gw), BF16),
        scratch_shapes=[pltpu.VMEM((nh, WC_HD + ONES, t), BF16)],
        compiler_params=_params(
            ("parallel", "parallel", "arbitrary"),
            2 * nh * 2 * (3 * tq * gw + 2 * t * WC_HD)
            + nh * (WC_HD + ONES) * t * 2
            + 2 * nh * (tq + 2 * WINDOW + n_ctx) * WC_G * tq * 4),
        name="swa_attn",
    )(qkvz, qkvz, qkvz, qkvz, sink_rows)


HALO = 8


def _pool_kernel(x_ref, xp_ref, xn_ref, mod_ref, g_ref, w_ref, wg_ref, bg_ref, sc_ref,
                 o_ref, hb_ref, s2_ref, s4_ref, s8_ref, *, tm, seq, n_ctx, n_lat_tiles):
    i = pl.program_id(1)
    is_ctx = i >= n_lat_tiles
    shift = jnp.where(is_ctx, mod_ref[0, 3:4, :], mod_ref[0, 0:1, :])
    scale = jnp.where(is_ctx, mod_ref[0, 4:5, :], mod_ref[0, 1:2, :])
    g = g_ref[...]
    has_prev = jnp.logical_and(i != 0, i != n_lat_tiles)
    has_next = jnp.logical_and(i != n_lat_tiles - 1, i < n_lat_tiles)
    h = _norm_mod(x_ref[0], g, shift, scale)
    hb_ref[HALO:HALO + tm, :] = h
    hb_ref[0:HALO, :] = jnp.where(has_prev, _norm_mod(xp_ref[0], g, shift, scale), 0.0)
    hb_ref[HALO + tm:, :] = jnp.where(has_next, _norm_mod(xn_ref[0], g, shift, scale), 0.0)
    n = tm + 2 * HALO
    s2_ref[1:n, :] = hb_ref[0:n - 1, :] + hb_ref[1:n, :]
    s4_ref[2:n - 1, :] = s2_ref[1:n - 2, :] + s2_ref[3:n, :]
    s8_ref[4:n - 3, :] = s4_ref[2:n - 5, :] + s4_ref[6:n - 1, :]
    sums = (s2_ref[HALO:HALO + tm, :], s4_ref[HALO:HALO + tm, :], s8_ref[HALO:HALO + tm, :],
            s8_ref[HALO - 4:HALO - 4 + tm, :] + s8_ref[HALO + 4:HALO + 4 + tm, :])
    pos = jnp.where(is_ctx, 0, i * tm) + lax.broadcasted_iota(jnp.int32, (tm, 1), 0)
    seg_len = jnp.where(is_ctx, n_ctx, seq)
    hb16 = h.astype(BF16)
    d_inner = len(POOL_WINDOWS) * POOL_GW
    for gi, w in enumerate(POOL_WINDOWS):
        lo = jnp.maximum(pos - w // 2, 0)
        hi = jnp.minimum(pos - w // 2 + w, seg_len)
        cnt = (hi - lo).astype(F32)
        hd = (sums[gi] / cnt - h).astype(BF16)
        cols = slice(gi * POOL_GW, (gi + 1) * POOL_GW)
        d = jnp.dot(hd, w_ref[:, cols], preferred_element_type=F32)
        y = jnp.dot(d.astype(BF16), wg_ref[gi], preferred_element_type=F32) + bg_ref[:, cols]
        y = y * sc_ref[:, cols]
        z = jnp.dot(hb16, w_ref[:, d_inner + gi * POOL_GW:d_inner + (gi + 1) * POOL_GW],
                    preferred_element_type=F32)
        o_ref[0, :, cols] = (y * _silu(z)).astype(o_ref.dtype)


def _pool_mixer(xa, mod, g, w_in, w_grp, b_grp, scale, *, seq, n_ctx, tm=POOL_TM):
    b, t, d = xa.shape
    d_inner = len(POOL_WINDOWS) * POOL_GW
    hb = tm // HALO
    last = t // HALO - 1
    kern = functools.partial(_pool_kernel, tm=tm, seq=seq, n_ctx=n_ctx, n_lat_tiles=seq // tm)
    buf = pltpu.VMEM((tm + 2 * HALO, d), F32)
    return pl.pallas_call(
        kern,
        grid=(b, t // tm),
        in_specs=[
            pl.BlockSpec((1, tm, d), lambda bi, i: (bi, i, 0)),
            pl.BlockSpec((1, HALO, d), lambda bi, i: (bi, jnp.maximum(i * hb - 1, 0), 0)),
            pl.BlockSpec((1, HALO, d), lambda bi, i: (bi, jnp.minimum((i + 1) * hb, last), 0)),
            pl.BlockSpec((1, 8, d), lambda bi, i: (bi, 0, 0)),
            pl.BlockSpec((1, d), lambda bi, i: (0, 0)),
            pl.BlockSpec((d, 2 * d_inner), lambda bi, i: (0, 0)),
            pl.BlockSpec((len(POOL_WINDOWS), POOL_GW, POOL_GW), lambda bi, i: (0, 0, 0)),
            pl.BlockSpec((1, d_inner), lambda bi, i: (0, 0)),
            pl.BlockSpec((1, d_inner), lambda bi, i: (0, 0)),
        ],
        out_specs=pl.BlockSpec((1, tm, d_inner), lambda bi, i: (bi, i, 0)),
        out_shape=jax.ShapeDtypeStruct((b, t, d_inner), BF16),
        scratch_shapes=[buf, buf, buf, buf],
        compiler_params=_params(
            ("parallel", "arbitrary"),
            2 * (tm * d * 4 + 2 * d * d_inner * 2 + d_inner * POOL_GW * 2 + tm * d_inner * 2)
            + 4 * (tm + 2 * HALO) * d * 4
            + 6 * tm * d * 4),
        name="pool_mixer",
    )(xa, xa, xa, mod, g, w_in, w_grp, b_grp, scale)


def _out_kernel(g_ref, w_ref, x_ref, mod_ref, fg_ref, o_ref, *, tm, seq, final):
    i = pl.program_id(1)
    y = jnp.dot(g_ref[0], w_ref[...], preferred_element_type=F32)
    row = i * tm + lax.broadcasted_iota(jnp.int32, (tm, 1), 0)
    gate = jnp.where(row >= seq, mod_ref[0, 5:6, :], mod_ref[0, 2:3, :])
    xn = x_ref[0] + gate * y
    if final:
        xn = xn * lax.rsqrt(jnp.mean(xn * xn, axis=-1, keepdims=True) + EPS) * fg_ref[...]
    o_ref[0] = xn


def _out_proj(gact, w, xa, mod, final_g, *, seq, final, tm):
    b, _, d_inner = gact.shape
    d = w.shape[1]
    rows = seq if final else xa.shape[1]
    kern = functools.partial(_out_kernel, tm=tm, seq=seq, final=final)
    return pl.pallas_call(
        kern,
        grid=(b, rows // tm),
        in_specs=[
            pl.BlockSpec((1, tm, d_inner), lambda bi, i: (bi, i, 0)),
            pl.BlockSpec((d_inner, d), lambda bi, i: (0, 0)),
            pl.BlockSpec((1, tm, d), lambda bi, i: (bi, i, 0)),
            pl.BlockSpec((1, 8, d), lambda bi, i: (bi, 0, 0)),
            pl.BlockSpec((1, d), lambda bi, i: (0, 0)),
        ],
        out_specs=pl.BlockSpec((1, tm, d), lambda bi, i: (bi, i, 0)),
        out_shape=jax.ShapeDtypeStruct((b, rows, d), F32),
        compiler_params=_params(
            ("parallel", "parallel"),
            2 * (tm * d_inner * 2 + d_inner * d * 2 + 2 * tm * d * 4) + 2 * tm * d * 4),
        name="out_proj",
    )(gact, w, xa, mod, final_g)


def kernel(x, c, ctx, c_ctx, norm_g, w_ada, b_ada, a_w_in, a_w_out, a_lam_q1, a_lam_k1, a_lam_q2, a_lam_k2, a_subln_g, b_w_in, b_w_grp, b_b_grp, b_scale, b_w_out, c_w_in, c_sink, c_w_out, final_g):
    b, seq, d = x.shape
    n_ctx = ctx.shape[1]
    depth = w_ada.shape[0]
    xa = jnp.concatenate([x, ctx], axis=1)

    cond8 = jnp.concatenate([c, c_ctx[None, :], jnp.zeros((8 - b - 1, d), F32)], axis=0)
    ada = _adaln(cond8, w_ada, b_ada)

    cos_a, sin_a = _rope_tables(seq, n_ctx, DA_HD)
    cos_c, sin_c = _rope_tables(seq, n_ctx, WC_HD)

    out = None
    for i in range(depth):
        m = i % N_MIXERS
        j = i // N_MIXERS
        last = i == depth - 1
        lat = ada[i, :b].reshape(b, 3, d)
        cm = jnp.broadcast_to(ada[i, b].reshape(1, 3, d), (b, 3, d))
        mod = jnp.concatenate([lat, cm, jnp.zeros((b, 2, d), F32)], axis=1)
        g = norm_g[i].reshape(1, d)

        if m == 0:
            lam_init = 0.8 - 0.6 * math.exp(-0.3 * i)
            n_qk = DA_HEADS * 2 * DA_HD
            w_in = _permute_rope_columns(a_w_in[j], 2 * n_qk, DA_HD).astype(BF16)
            sections = (("Q", n_qk), ("K", n_qk), ("P", 2 * DA_HEADS * DA_VD))
            qkvz = _proj(xa, mod, g, w_in, cos_a, sin_a, seq=seq, sections=sections,
                         qscale=DA_HD ** -0.5 * LOG2E)
            lamp = jnp.stack([a_lam_q1[j], a_lam_k1[j], a_lam_q2[j], a_lam_k2[j]], axis=0)
            sg = a_subln_g[j].reshape(1, DA_VD)
            gact = _diff_attn(qkvz, lamp, sg, seq=seq, n_ctx=n_ctx, lam_init=lam_init,
                              ctx_queries=not last)
            w_out = a_w_out[j]
        elif m == 1:
            gact = _pool_mixer(xa, mod, g, b_w_in[j].astype(BF16), b_w_grp[j].astype(BF16),
                               b_b_grp[j].reshape(1, -1), b_scale[j].reshape(1, -1),
                               seq=seq, n_ctx=n_ctx)
            w_out = b_w_out[j]
        else:
            n_q, n_k = WC_KV * WC_G * WC_HD, WC_KV * WC_HD
            w_in = _permute_rope_columns(c_w_in[j], n_q + n_k, WC_HD)
            w_in = jnp.concatenate([w_in[:, :n_q], w_in[:, n_q + 2 * n_k:],
                                    w_in[:, n_q:n_q + 2 * n_k]], axis=1).astype(BF16)
            sections = (("Q", n_q), ("P", n_q), ("K", n_k), ("P", n_k))
            qkvz = _proj(xa, mod, g, w_in, cos_c, sin_c, seq=seq, sections=sections,
                         qscale=WC_HD ** -0.5 * LOG2E)
            sink_rows = jnp.repeat(c_sink[j].astype(F32) * LOG2E, SWA_TQ).reshape(
                WC_KV, 1, WC_G * SWA_TQ)
            gact = _swa_attn(qkvz, sink_rows, seq=seq, n_ctx=n_ctx)
            w_out = c_w_out[j]

        res = _out_proj(gact, w_out.astype(BF16), xa, mod, final_g.reshape(1, d), seq=seq,
                        final=last, tm=OUT_TM_FINAL if last else OUT_TM)
        if last:
            out = res
        else:
            xa = res
    return out
```

```python
import functools
import math

import jax
import jax.numpy as jnp
from jax import lax
from jax.experimental import pallas as pl
from jax.experimental.pallas import tpu as pltpu

F32 = jnp.float32
BF16 = jnp.bfloat16

GRID_W = 64
EPS = 1e-6
ROPE_BASE = 10000.0
N_MIXERS = 3
DA_HEADS = 16
DA_HD = 64
DA_VD = 128
POOL_WINDOWS = (2, 4, 8, 16)
POOL_GW = 512
WC_HD = 128
WC_KV = 4
WC_G = 4
WINDOW = 128

LOG2E = 1.4426950408889634
LANE = 128
SUB = 256
V7X_VMEM_BYTES = 64 * 1024 * 1024

PROJ_TM, PROJ_TN = 1088, 1024
OUT_TM = 1088
OUT_TM_FINAL = 1024
POOL_TM = 256
ATTN_TQ, ATTN_TK = 512, 512
ATTN_NSUB = 4
SWA_TQ, SWA_NH = 256, 4


def _params(semantics, vmem_bytes):
    assert vmem_bytes <= V7X_VMEM_BYTES
    return pltpu.CompilerParams(dimension_semantics=semantics, vmem_limit_bytes=int(vmem_bytes))


def _silu(v):
    return v * jax.nn.sigmoid(v)


def _adaln_kernel(c_ref, w_ref, b_ref, o_ref):
    a = _silu(c_ref[...]).astype(BF16)
    o_ref[0] = jnp.dot(a, w_ref[0].astype(BF16), preferred_element_type=F32) + b_ref[0]


def _adaln(cond8, w_ada, b_ada):
    depth, d, d3 = w_ada.shape
    nt = d3 // d
    return pl.pallas_call(
        _adaln_kernel,
        grid=(depth, nt),
        in_specs=[
            pl.BlockSpec((8, d), lambda l, n: (0, 0)),
            pl.BlockSpec((1, d, d), lambda l, n: (l, 0, n)),
            pl.BlockSpec((1, 1, d), lambda l, n: (l, 0, n)),
        ],
        out_specs=pl.BlockSpec((1, 8, d), lambda l, n: (l, 0, n)),
        out_shape=jax.ShapeDtypeStruct((depth, 8, d3), F32),
        compiler_params=_params(("parallel", "parallel"), 2 * (d * d * 4 + 18 * d * 4) + d * d * 2),
        name="adaln",
    )(cond8, w_ada, b_ada.reshape(depth, 1, d3))


def _norm_mod(xf, g, shift, scale):
    inv = lax.rsqrt(jnp.mean(xf * xf, axis=-1, keepdims=True) + EPS)
    return xf * inv * (g * (1.0 + scale)) + shift


HALF = LANE // 2


def _rope_lane_layout(head_dim):
    quarter = head_dim // 4
    lane = jnp.arange(LANE)
    second = lane // HALF
    grp = (lane % HALF) // quarter
    freq = lane % quarter
    axis = grp % 2
    unit = grp // 2
    old = unit * head_dim + (axis * 2 + second) * quarter + freq
    sign = jnp.where(second == 0, -1.0, 1.0).astype(F32)
    return old, axis, freq, sign


def _rope_tables(seq, n_ctx, head_dim):
    quarter = head_dim // 4
    _, axis, freq, sign = _rope_lane_layout(head_dim)
    inv = ROPE_BASE ** (-jnp.arange(quarter, dtype=F32) / quarter)
    pos = jnp.arange(seq)
    coord = jnp.where(axis[None, :] == 0, (pos // GRID_W)[:, None], (pos % GRID_W)[:, None])
    ang = coord.astype(F32) * inv[freq][None, :]
    cos = jnp.concatenate([jnp.cos(ang), jnp.ones((n_ctx, LANE), F32)], axis=0)
    sin = jnp.concatenate([jnp.sin(ang) * sign[None, :], jnp.zeros((n_ctx, LANE), F32)], axis=0)
    return cos, sin


def _permute_rope_columns(w, n_cols, head_dim):
    quarter = head_dim // 4
    units = LANE // head_dim
    d = w.shape[0]
    head = w[:, :n_cols].reshape(d, n_cols // LANE, units, 2, 2, quarter)
    head = head.transpose(0, 1, 4, 2, 3, 5).reshape(d, n_cols)
    return jnp.concatenate([head, w[:, n_cols:]], axis=1)


def _proj_kernel(x_ref, mod_ref, g_ref, w_ref, cos_ref, sin_ref, o_ref, h_ref,
                 *, tm, seq, groups, qscale):
    i = pl.program_id(1)
    j = pl.program_id(2)

    def normalise(shift, scale):
        h_ref[...] = _norm_mod(x_ref[0], g_ref[...], shift, scale).astype(BF16)

    @pl.when(j == 0)
    def _():
        all_latent = (i + 1) * tm <= seq
        all_context = i * tm >= seq

        @pl.when(all_latent)
        def _():
            normalise(mod_ref[0, 0:1, :], mod_ref[0, 1:2, :])

        @pl.when(all_context)
        def _():
            normalise(mod_ref[0, 3:4, :], mod_ref[0, 4:5, :])

        @pl.when(jnp.logical_not(jnp.logical_or(all_latent, all_context)))
        def _():
            is_ctx = i * tm + lax.broadcasted_iota(jnp.int32, (tm, 1), 0) >= seq
            normalise(jnp.where(is_ctx, mod_ref[0, 3:4, :], mod_ref[0, 0:1, :]),
                      jnp.where(is_ctx, mod_ref[0, 4:5, :], mod_ref[0, 1:2, :]))

    def emit(kinds):
        tables = {}
        for kind in set(kinds) - {"P"}:
            mul = qscale if kind == "Q" else 1.0
            tables[kind] = (cos_ref[...] * mul, sin_ref[...] * mul)
        for s, kind in enumerate(kinds):
            acc = jnp.dot(h_ref[...], w_ref[:, s * SUB:(s + 1) * SUB],
                          preferred_element_type=F32)
            if kind == "P":
                o_ref[0, :, s * SUB:(s + 1) * SUB] = acc.astype(o_ref.dtype)
                continue
            cos, sin = tables[kind]
            for u in range(SUB // LANE):
                t = acc[:, u * LANE:(u + 1) * LANE]
                c0 = s * SUB + u * LANE
                o_ref[0, :, c0:c0 + LANE] = (t * cos + pltpu.roll(t, HALF, 1) * sin
                                             ).astype(o_ref.dtype)

    for j_lo, j_hi, kinds in groups:
        pl.when(jnp.logical_and(j >= j_lo, j <= j_hi))(functools.partial(emit, kinds))


def _proj(xa, mod, g, w, cos, sin, *, seq, sections, qscale, tm=PROJ_TM, tn=PROJ_TN):
    b, t, d = xa.shape
    n = w.shape[1]
    per_tile = tn // SUB
    kinds = [kind for kind, cols in sections for _ in range(cols // SUB)]
    assert len(kinds) * SUB == n
    tiles = [tuple(kinds[jt * per_tile:(jt + 1) * per_tile]) for jt in range(n // tn)]
    groups = []
    for jt, kt in enumerate(tiles):
        if groups and groups[-1][2] == kt:
            groups[-1] = (groups[-1][0], jt, kt)
        else:
            groups.append((jt, jt, kt))
    kern = functools.partial(_proj_kernel, tm=tm, seq=seq, groups=tuple(groups), qscale=qscale)
    return pl.pallas_call(
        kern,
        grid=(b, t // tm, n // tn),
        in_specs=[
            pl.BlockSpec((1, tm, d), lambda bi, i, j: (bi, i, 0)),
            pl.BlockSpec((1, 8, d), lambda bi, i, j: (bi, 0, 0)),
            pl.BlockSpec((1, d), lambda bi, i, j: (0, 0)),
            pl.BlockSpec((d, tn), lambda bi, i, j: (0, j)),
            pl.BlockSpec((tm, LANE), lambda bi, i, j: (i, 0)),
            pl.BlockSpec((tm, LANE), lambda bi, i, j: (i, 0)),
        ],
        out_specs=pl.BlockSpec((1, tm, tn), lambda bi, i, j: (bi, i, j)),
        out_shape=jax.ShapeDtypeStruct((b, t, n), BF16),
        scratch_shapes=[pltpu.VMEM((tm, d), BF16)],
        compiler_params=_params(("parallel", "parallel", "arbitrary"),
                                2 * (tm * d * 4 + d * tn * 2 + tm * tn * 2) + tm * d * 2
                                + 2 * tm * SUB * 4),
        name="proj",
    )(xa, mod, g, w, cos, sin)


PEAK_LIMIT = 2.0 ** 64
ONES = 16


def _stage_vt(v_ref, vt_ref):
    rows, vd = v_ref.shape[1], v_ref.shape[2]
    vt_ref[0:vd, :] = v_ref[0].T
    vt_ref[vd:vd + ONES, :] = jnp.ones((ONES, rows), vt_ref.dtype)


def _diff_attn_kernel(lamp_ref, q_ref, k_ref, v_ref, z_ref, sg_ref, o_ref, acc_ref, vt_ref,
                      *, tq, nsub, tk, chunks, lam_init):
    @pl.when(pl.program_id(2) == 0)
    def _():
        _stage_vt(v_ref, vt_ref)

    lp = lamp_ref[...]
    lam = (jnp.exp(jnp.sum(lp[0:1] * lp[1:2], axis=1, keepdims=True))
           - jnp.exp(jnp.sum(lp[2:3] * lp[3:4], axis=1, keepdims=True)) + lam_init)

    def queries(u):
        q = q_ref[0, u * tq:(u + 1) * tq, :].astype(F32)
        lane = lax.broadcasted_iota(jnp.int32, q.shape, 1)
        map0 = lane % HALF < HALF // 2
        return jnp.concatenate([jnp.where(map0, q, 0.0),
                                jnp.where(map0, 0.0, q)], axis=0).astype(BF16)

    def scores(qm, c0, cl):
        return lax.dot_general(k_ref[0, pl.ds(c0, cl), :], qm, (((1,), (1,)), ((), ())),
                               preferred_element_type=F32)

    def pv(e, c0, cl):
        return jnp.dot(vt_ref[:, pl.ds(c0, cl)], e.astype(BF16),
                       preferred_element_type=F32)

    def finish(u):
        acc = acc_ref[u]
        inv = 1.0 / acc[DA_VD:DA_VD + 1, :]
        o_t = (acc[:DA_VD, :tq] * inv[:, :tq]
               - lam * (acc[:DA_VD, tq:] * inv[:, tq:]))
        o = o_t.T
        o = o * lax.rsqrt(jnp.mean(o * o, axis=-1, keepdims=True) + EPS) * sg_ref[...]
        o = o * (1.0 - lam_init)
        z = z_ref[0, u * tq:(u + 1) * tq, :].astype(F32)
        o_ref[0, u * tq:(u + 1) * tq, :] = (o * _silu(z)).astype(o_ref.dtype)

    def fast(u, qm):
        m = peak = None
        for c0, cl in chunks:
            s = scores(qm, c0, cl)
            if m is None:
                m = jnp.max(s[0:8], axis=0, keepdims=True)
            e = jnp.exp2(s - m).astype(BF16)
            emax = jnp.max(e, axis=0, keepdims=True).astype(F32)
            up = jnp.maximum(emax, 1.0)
            alpha = 1.0 / up
            if peak is None:
                acc_ref[u] = pv(e, c0, cl) * alpha
                peak = emax
            else:
                acc_ref[u] = (acc_ref[u] + pv(e, c0, cl)) * alpha
                peak = jnp.maximum(peak, emax)
            m = m + jnp.log2(up)
        finish(u)
        return jnp.max(peak)

    def exact(u, qm):
        s = scores(qm, *chunks[0])
        m = jnp.max(s, axis=0, keepdims=True)
        acc_ref[u] = pv(jnp.exp2(s - m), *chunks[0])

        def body(c, m):
            c0 = pl.multiple_of(chunks[1][0] + c * tk, tk)
            s = scores(qm, c0, tk)
            m_new = jnp.maximum(m, jnp.max(s, axis=0, keepdims=True))
            acc_ref[u] = jnp.exp2(m - m_new) * acc_ref[u] + pv(jnp.exp2(s - m_new), c0, tk)
            return m_new

        if len(chunks) > 1:
            lax.fori_loop(0, len(chunks) - 1, body, m)
        finish(u)

    qms = [queries(u) for u in range(nsub)]
    peak = functools.reduce(jnp.maximum, [fast(u, qms[u]) for u in range(nsub)])

    @pl.when(jnp.logical_not(peak <= PEAK_LIMIT))
    def _():
        for u in range(nsub):
            exact(u, qms[u])


def _diff_attn(qkvz, lamp, sg, *, seq, n_ctx, lam_init, ctx_queries,
               tq=ATTN_TQ, nsub=ATTN_NSUB, tk=ATTN_TK):
    b, t, _ = qkvz.shape
    h = DA_HEADS
    if ctx_queries:
        tq, nsub, chunks = n_ctx, 1, ((0, n_ctx),)
        q_blk0, k_rows, k_blk = seq // n_ctx, n_ctx, seq // n_ctx
        out_rows = n_ctx
    else:
        chunks = ((seq, n_ctx),) + tuple((c, tk) for c in range(0, seq, tk))
        q_blk0, k_rows, k_blk = 0, t, 0
        out_rows = seq
    kern = functools.partial(_diff_attn_kernel, tq=tq, nsub=nsub, tk=tk, chunks=chunks,
                             lam_init=lam_init)
    tqs = tq * nsub
    return pl.pallas_call(
        kern,
        grid=(b, h, out_rows // tqs),
        in_specs=[
            pl.BlockSpec((4, DA_HD), lambda bi, hi, i: (0, 0)),
            pl.BlockSpec((1, tqs, LANE), lambda bi, hi, i: (bi, q_blk0 + i, hi)),
            pl.BlockSpec((1, k_rows, LANE), lambda bi, hi, i: (bi, k_blk, h + hi)),
            pl.BlockSpec((1, k_rows, LANE), lambda bi, hi, i: (bi, k_blk, 2 * h + hi)),
            pl.BlockSpec((1, tqs, LANE), lambda bi, hi, i: (bi, q_blk0 + i, 3 * h + hi)),
            pl.BlockSpec((1, DA_VD), lambda bi, hi, i: (0, 0)),
        ],
        out_specs=pl.BlockSpec((1, tqs, LANE), lambda bi, hi, i: (bi, i, hi)),
        out_shape=jax.ShapeDtypeStruct((b, out_rows, h * DA_VD), BF16),
        scratch_shapes=[pltpu.VMEM((nsub, DA_VD + ONES, 2 * tq), F32),
                        pltpu.VMEM((DA_VD + ONES, k_rows), BF16)],
        compiler_params=_params(
            ("parallel", "parallel", "arbitrary"),
            2 * LANE * 2 * (3 * tqs + 2 * k_rows)
            + (DA_VD + ONES) * (nsub * 2 * tq * 4 + k_rows * 2)
            + 4 * tk * 2 * tq * 4),
        name="diff_attn_ctx" if ctx_queries else "diff_attn",
    )(lamp, qkvz, qkvz, qkvz, qkvz, sg)


def _swa_kernel(q_ref, k_ref, v_ref, z_ref, sink_ref, o_ref, vt_ref,
                *, tq, nh, seq, n_ctx, n_lat_tiles):
    i = pl.program_id(2)
    gw = WC_G * WC_HD

    @pl.when(i == 0)
    def _():
        for hh in range(nh):
            vt_ref[hh, 0:WC_HD, :] = v_ref[0, :, hh * WC_HD:(hh + 1) * WC_HD].T
            vt_ref[hh, WC_HD:WC_HD + ONES, :] = jnp.ones((ONES, v_ref.shape[1]), vt_ref.dtype)

    span = tq + 2 * WINDOW
    nt = (((1,), (1,)), ((), ()))

    def head(hh):
        q4 = jnp.concatenate([q_ref[0, :, hh * gw + g * WC_HD:hh * gw + (g + 1) * WC_HD]
                              for g in range(WC_G)], axis=0)
        k_cols = slice(hh * WC_HD, (hh + 1) * WC_HD)
        s_ctx = lax.dot_general(k_ref[0, seq:seq + n_ctx, k_cols], q4, nt,
                                preferred_element_type=F32)
        return q4, k_cols, s_ctx, sink_ref[hh]

    def finish(hh, o_t, den):
        o_t = o_t * (1.0 / den)
        for g in range(WC_G):
            cols = slice(hh * gw + g * WC_HD, hh * gw + (g + 1) * WC_HD)
            o = o_t[:, g * tq:(g + 1) * tq].T
            o_ref[0, :, cols] = (o * _silu(z_ref[0, :, cols].astype(F32))).astype(o_ref.dtype)

    def fast(hh, s_ctx, sink, s_lat, base):
        e_ctx = jnp.exp2(s_ctx).astype(BF16)
        acc = jnp.dot(vt_ref[hh, :, seq:seq + n_ctx], e_ctx, preferred_element_type=F32)
        peak = jnp.max(e_ctx, axis=0, keepdims=True).astype(F32)
        if s_lat is not None:
            e_lat = jnp.exp2(s_lat).astype(BF16)
            acc = acc + jnp.dot(vt_ref[hh, :, pl.ds(base, span)], e_lat,
                                preferred_element_type=F32)
            peak = jnp.maximum(peak, jnp.max(e_lat, axis=0, keepdims=True).astype(F32))
        e_sink = jnp.exp2(sink)
        den = acc[WC_HD:WC_HD + 1, :] + e_sink
        finish(hh, acc[:WC_HD, :], den)
        return jnp.logical_and(jnp.max(jnp.maximum(peak, e_sink)) <= PEAK_LIMIT,
                               jnp.min(den) >= 1.0 / PEAK_LIMIT)

    def exact(hh, s_ctx, sink, s_lat, base):
        m = jnp.maximum(jnp.max(s_ctx, axis=0, keepdims=True), sink)
        if s_lat is not None:
            m = jnp.maximum(m, jnp.max(s_lat, axis=0, keepdims=True))
        e_c = jnp.exp2(s_ctx - m)
        den = jnp.sum(e_c, axis=0, keepdims=True) + jnp.exp2(sink - m)
        o_t = jnp.dot(vt_ref[hh, :WC_HD, seq:seq + n_ctx], e_c.astype(BF16),
                      preferred_element_type=F32)
        if s_lat is not None:
            e_l = jnp.exp2(s_lat - m)
            den = den + jnp.sum(e_l, axis=0, keepdims=True)
            o_t = o_t + jnp.dot(vt_ref[hh, :WC_HD, pl.ds(base, span)], e_l.astype(BF16),
                                preferred_element_type=F32)
        finish(hh, o_t, den)

    def tile(latent):
        if latent:
            start = i * tq
            base = pl.multiple_of(jnp.clip(start - WINDOW, 0, seq - span), LANE)
            kpos = base + lax.broadcasted_iota(jnp.int32, (span, 1), 0)
            qpos = start + lax.broadcasted_iota(jnp.int32, (1, tq), 1)
            bias = jnp.where(jnp.abs(qpos - kpos) <= WINDOW, 0.0, -jnp.inf)
            bias = jnp.concatenate([bias] * WC_G, axis=1)
        args = []
        for hh in range(nh):
            q4, k_cols, s_ctx, sink = head(hh)
            s_lat = None
            if latent:
                s_lat = lax.dot_general(k_ref[0, pl.ds(base, span), k_cols], q4, nt,
                                        preferred_element_type=F32) + bias
            args.append((hh, s_ctx, sink, s_lat, base if latent else None))
        in_range = functools.reduce(jnp.logical_and, [fast(*a) for a in args])

        @pl.when(jnp.logical_not(in_range))
        def _():
            for a in args:
                exact(*a)

    pl.when(i < n_lat_tiles)(functools.partial(tile, True))
    pl.when(i >= n_lat_tiles)(functools.partial(tile, False))


def _swa_attn(qkvz, sink_rows, *, seq, n_ctx, tq=SWA_TQ, nh=SWA_NH):
    b, t, _ = qkvz.shape
    gw = WC_G * WC_HD
    n_lat = seq // tq
    kern = functools.partial(_swa_kernel, tq=tq, nh=nh, seq=seq, n_ctx=n_ctx, n_lat_tiles=n_lat)
    z_col0, k_col0, v_col0 = WC_KV * gw, 2 * WC_KV * gw, 2 * WC_KV * gw + WC_KV * WC_HD
    assert k_col0 % (nh * WC_HD) == 0 and v_col0 % (nh * WC_HD) == 0 and z_col0 % (nh * gw) == 0
    k_blk0, v_blk0, z_blk0 = k_col0 // (nh * WC_HD), v_col0 // (nh * WC_HD), z_col0 // (nh * gw)
    return pl.pallas_call(
        kern,
        grid=(b, WC_KV // nh, t // tq),
        in_specs=[
            pl.BlockSpec((1, tq, nh * gw), lambda bi, n, i: (bi, i, n)),
            pl.BlockSpec((1, t, nh * WC_HD), lambda bi, n, i: (bi, 0, k_blk0 + n)),
            pl.BlockSpec((1, t, nh * WC_HD), lambda bi, n, i: (bi, 0, v_blk0 + n)),
            pl.BlockSpec((1, tq, nh * gw), lambda bi, n, i: (bi, i, z_blk0 + n)),
            pl.BlockSpec((nh, 1, WC_G * tq), lambda bi, n, i: (n, 0, 0)),
        ],
        out_specs=pl.BlockSpec((1, tq, nh * gw), lambda bi, n, i: (bi, i, n)),
        out_shape=jax.ShapeDtypeStruct((b, t, WC_KV * gw), BF16),
        scratch_shapes=[pltpu.VMEM((nh, WC_HD + ONES, t), BF16)],
        compiler_params=_params(
            ("parallel", "parallel", "arbitrary"),
            2 * nh * 2 * (3 * tq * gw + 2 * t * WC_HD)
            + nh * (WC_HD + ONES) * t * 2
            + 2 * nh * (tq + 2 * WINDOW + n_ctx) * WC_G * tq * 4),
        name="swa_attn",
    )(qkvz, qkvz, qkvz, qkvz, sink_rows)


HALO = 8


def _pool_kernel(x_ref, xp_ref, xn_ref, mod_ref, g_ref, w_ref, wg_ref, bg_ref, sc_ref,
                 o_ref, hb_ref, s2_ref, s4_ref, s8_ref, *, tm, seq, n_ctx, n_lat_tiles):
    i = pl.program_id(1)
    is_ctx = i >= n_lat_tiles
    shift = jnp.where(is_ctx, mod_ref[0, 3:4, :], mod_ref[0, 0:1, :])
    scale = jnp.where(is_ctx, mod_ref[0, 4:5, :], mod_ref[0, 1:2, :])
    g = g_ref[...]
    has_prev = jnp.logical_and(i != 0, i != n_lat_tiles)
    has_next = jnp.logical_and(i != n_lat_tiles - 1, i < n_lat_tiles)
    h = _norm_mod(x_ref[0], g, shift, scale)
    hb_ref[HALO:HALO + tm, :] = h
    hb_ref[0:HALO, :] = jnp.where(has_prev, _norm_mod(xp_ref[0], g, shift, scale), 0.0)
    hb_ref[HALO + tm:, :] = jnp.where(has_next, _norm_mod(xn_ref[0], g, shift, scale), 0.0)
    n = tm + 2 * HALO
    s2_ref[1:n, :] = hb_ref[0:n - 1, :] + hb_ref[1:n, :]
    s4_ref[2:n - 1, :] = s2_ref[1:n - 2, :] + s2_ref[3:n, :]
    s8_ref[4:n - 3, :] = s4_ref[2:n - 5, :] + s4_ref[6:n - 1, :]
    sums = (s2_ref[HALO:HALO + tm, :], s4_ref[HALO:HALO + tm, :], s8_ref[HALO:HALO + tm, :],
            s8_ref[HALO - 4:HALO - 4 + tm, :] + s8_ref[HALO + 4:HALO + 4 + tm, :])
    pos = jnp.where(is_ctx, 0, i * tm) + lax.broadcasted_iota(jnp.int32, (tm, 1), 0)
    seg_len = jnp.where(is_ctx, n_ctx, seq)
    hb16 = h.astype(BF16)
    d_inner = len(POOL_WINDOWS) * POOL_GW
    for gi, w in enumerate(POOL_WINDOWS):
        lo = jnp.maximum(pos - w // 2, 0)
        hi = jnp.minimum(pos - w // 2 + w, seg_len)
        cnt = (hi - lo).astype(F32)
        hd = (sums[gi] / cnt - h).astype(BF16)
        cols = slice(gi * POOL_GW, (gi + 1) * POOL_GW)
        d = jnp.dot(hd, w_ref[:, cols], preferred_element_type=F32)
        y = jnp.dot(d.astype(BF16), wg_ref[gi], preferred_element_type=F32) + bg_ref[:, cols]
        y = y * sc_ref[:, cols]
        z = jnp.dot(hb16, w_ref[:, d_inner + gi * POOL_GW:d_inner + (gi + 1) * POOL_GW],
                    preferred_element_type=F32)
        o_ref[0, :, cols] = (y * _silu(z)).astype(o_ref.dtype)


def _pool_mixer(xa, mod, g, w_in, w_grp, b_grp, scale, *, seq, n_ctx, tm=POOL_TM):
    b, t, d = xa.shape
    d_inner = len(POOL_WINDOWS) * POOL_GW
    hb = tm // HALO
    last = t // HALO - 1
    kern = functools.partial(_pool_kernel, tm=tm, seq=seq, n_ctx=n_ctx, n_lat_tiles=seq // tm)
    buf = pltpu.VMEM((tm + 2 * HALO, d), F32)
    return pl.pallas_call(
        kern,
        grid=(b, t // tm),
        in_specs=[
            pl.BlockSpec((1, tm, d), lambda bi, i: (bi, i, 0)),
            pl.BlockSpec((1, HALO, d), lambda bi, i: (bi, jnp.maximum(i * hb - 1, 0), 0)),
            pl.BlockSpec((1, HALO, d), lambda bi, i: (bi, jnp.minimum((i + 1) * hb, last), 0)),
            pl.BlockSpec((1, 8, d), lambda bi, i: (bi, 0, 0)),
            pl.BlockSpec((1, d), lambda bi, i: (0, 0)),
            pl.BlockSpec((d, 2 * d_inner), lambda bi, i: (0, 0)),
            pl.BlockSpec((len(POOL_WINDOWS), POOL_GW, POOL_GW), lambda bi, i: (0, 0, 0)),
            pl.BlockSpec((1, d_inner), lambda bi, i: (0, 0)),
            pl.BlockSpec((1, d_inner), lambda bi, i: (0, 0)),
        ],
        out_specs=pl.BlockSpec((1, tm, d_inner), lambda bi, i: (bi, i, 0)),
        out_shape=jax.ShapeDtypeStruct((b, t, d_inner), BF16),
        scratch_shapes=[buf, buf, buf, buf],
        compiler_params=_params(
            ("parallel", "arbitrary"),
            2 * (tm * d * 4 + 2 * d * d_inner * 2 + d_inner * POOL_GW * 2 + tm * d_inner * 2)
            + 4 * (tm + 2 * HALO) * d * 4
            + 6 * tm * d * 4),
        name="pool_mixer",
    )(xa, xa, xa, mod, g, w_in, w_grp, b_grp, scale)


def _out_kernel(g_ref, gc_ref, w_ref, x_ref, mod_ref, fg_ref, o_ref, *, tm, seq, final, split):
    i = pl.program_id(1)
    is_ctx = i * tm >= seq

    def emit(g, gate):
        y = jnp.dot(g, w_ref[...], preferred_element_type=F32)
        xn = x_ref[0] + gate * y
        if final:
            xn = xn * lax.rsqrt(jnp.mean(xn * xn, axis=-1, keepdims=True) + EPS) * fg_ref[...]
        o_ref[0] = xn

    if split:
        @pl.when(jnp.logical_not(is_ctx))
        def _():
            emit(g_ref[0], mod_ref[0, 2:3, :])

        @pl.when(is_ctx)
        def _():
            emit(gc_ref[0], mod_ref[0, 5:6, :])
    else:
        row = i * tm + lax.broadcasted_iota(jnp.int32, (tm, 1), 0)
        emit(g_ref[0], jnp.where(row >= seq, mod_ref[0, 5:6, :], mod_ref[0, 2:3, :]))


def _out_proj(gact, gctx, w, xa, mod, final_g, *, seq, final, tm):
    b, _, d_inner = gact.shape
    d = w.shape[1]
    rows = seq if final else xa.shape[1]
    split = gctx is not None
    if split:
        assert seq % tm == 0 and gctx.shape[1] == tm
        last_lat = seq // tm - 1
        g_map = lambda bi, i: (bi, jnp.minimum(i, last_lat), 0)
    else:
        gctx = gact
        g_map = lambda bi, i: (bi, i, 0)
    kern = functools.partial(_out_kernel, tm=tm, seq=seq, final=final, split=split)
    return pl.pallas_call(
        kern,
        grid=(b, rows // tm),
        in_specs=[
            pl.BlockSpec((1, tm, d_inner), g_map),
            pl.BlockSpec((1, tm, d_inner), lambda bi, i: (bi, 0, 0)),
            pl.BlockSpec((d_inner, d), lambda bi, i: (0, 0)),
            pl.BlockSpec((1, tm, d), lambda bi, i: (bi, i, 0)),
            pl.BlockSpec((1, 8, d), lambda bi, i: (bi, 0, 0)),
            pl.BlockSpec((1, d), lambda bi, i: (0, 0)),
        ],
        out_specs=pl.BlockSpec((1, tm, d), lambda bi, i: (bi, i, 0)),
        out_shape=jax.ShapeDtypeStruct((b, rows, d), F32),
        compiler_params=_params(
            ("parallel", "parallel"),
            2 * (2 * tm * d_inner * 2 + d_inner * d * 2 + 2 * tm * d * 4) + 2 * tm * d * 4),
        name="out_proj",
    )(gact, gctx, w, xa, mod, final_g)


def kernel(x, c, ctx, c_ctx, norm_g, w_ada, b_ada, a_w_in, a_w_out, a_lam_q1, a_lam_k1, a_lam_q2, a_lam_k2, a_subln_g, b_w_in, b_w_grp, b_b_grp, b_scale, b_w_out, c_w_in, c_sink, c_w_out, final_g):
    b, seq, d = x.shape
    n_ctx = ctx.shape[1]
    depth = w_ada.shape[0]
    xa = jnp.concatenate([x, ctx], axis=1)

    cond8 = jnp.concatenate([c, c_ctx[None, :], jnp.zeros((8 - b - 1, d), F32)], axis=0)
    ada = _adaln(cond8, w_ada, b_ada)

    cos_a, sin_a = _rope_tables(seq, n_ctx, DA_HD)
    cos_c, sin_c = _rope_tables(seq, n_ctx, WC_HD)

    out = None
    for i in range(depth):
        m = i % N_MIXERS
        j = i // N_MIXERS
        last = i == depth - 1
        lat = ada[i, :b].reshape(b, 3, d)
        cm = jnp.broadcast_to(ada[i, b].reshape(1, 3, d), (b, 3, d))
        mod = jnp.concatenate([lat, cm, jnp.zeros((b, 2, d), F32)], axis=1)
        g = norm_g[i].reshape(1, d)
        gctx = None

        if m == 0:
            lam_init = 0.8 - 0.6 * math.exp(-0.3 * i)
            n_qk = DA_HEADS * 2 * DA_HD
            w_in = _permute_rope_columns(a_w_in[j], 2 * n_qk, DA_HD).astype(BF16)
            sections = (("Q", n_qk), ("K", n_qk), ("P", 2 * DA_HEADS * DA_VD))
            qkvz = _proj(xa, mod, g, w_in, cos_a, sin_a, seq=seq, sections=sections,
                         qscale=DA_HD ** -0.5 * LOG2E)
            lamp = jnp.stack([a_lam_q1[j], a_lam_k1[j], a_lam_q2[j], a_lam_k2[j]], axis=0)
            sg = a_subln_g[j].reshape(1, DA_VD)
            gact = _diff_attn(qkvz, lamp, sg, seq=seq, n_ctx=n_ctx, lam_init=lam_init,
                              ctx_queries=False)
            if not last:
                gctx = _diff_attn(qkvz, lamp, sg, seq=seq, n_ctx=n_ctx, lam_init=lam_init,
                                  ctx_queries=True)
            w_out = a_w_out[j]
        elif m == 1:
            gact = _pool_mixer(xa, mod, g, b_w_in[j].astype(BF16), b_w_grp[j].astype(BF16),
                               b_b_grp[j].reshape(1, -1), b_scale[j].reshape(1, -1),
                               seq=seq, n_ctx=n_ctx)
            w_out = b_w_out[j]
        else:
            n_q, n_k = WC_KV * WC_G * WC_HD, WC_KV * WC_HD
            w_in = _permute_rope_columns(c_w_in[j], n_q + n_k, WC_HD)
            w_in = jnp.concatenate([w_in[:, :n_q], w_in[:, n_q + 2 * n_k:],
                                    w_in[:, n_q:n_q + 2 * n_k]], axis=1).astype(BF16)
            sections = (("Q", n_q), ("P", n_q), ("K", n_k), ("P", n_k))
            qkvz = _proj(xa, mod, g, w_in, cos_c, sin_c, seq=seq, sections=sections,
                         qscale=WC_HD ** -0.5 * LOG2E)
            sink_rows = jnp.repeat(c_sink[j].astype(F32) * LOG2E, SWA_TQ).reshape(
                WC_KV, 1, WC_G * SWA_TQ)
            gact = _swa_attn(qkvz, sink_rows, seq=seq, n_ctx=n_ctx)
            w_out = c_w_out[j]

        if last:
            tm = OUT_TM_FINAL
        else:
            tm = OUT_TM if gctx is None else n_ctx
        res = _out_proj(gact, gctx, w_out.astype(BF16), xa, mod, final_g.reshape(1, d), seq=seq,
                        final=last, tm=tm)
        if last:
            out = res
        else:
            xa = res
    return out
```
